```python
import jax, jax.numpy as jnp
from jax import lax
import numpy as np

D_MODEL = 1024
BATCH = 8
SEQ = 8192
DEPTH = 2

GRID_W = 64
CTX_LEN = 256
N_EVEN = (DEPTH + 1) // 2
N_ODD = DEPTH // 2
POOL_GROUPS = 4
POOL_WINDOWS = (2, 4, 8, 16)
POOL_DIM = D_MODEL // 2
POOL_GDIM = POOL_DIM // POOL_GROUPS
HEAD_DIM = 64
N_Q_HEADS = (D_MODEL // 2) // HEAD_DIM
N_KV_HEADS = 2
GQA_GROUP = N_Q_HEADS // N_KV_HEADS
Q_DIM = N_Q_HEADS * HEAD_DIM
KV_DIM = N_KV_HEADS * HEAD_DIM
WINDOW = 128
BLOCK = 128
ROPE_BASE = 10000.0
ROPE_HALF = HEAD_DIM // 2
IN_DIM = POOL_DIM + Q_DIM + 2 * KV_DIM
MIX_DIM = POOL_DIM + Q_DIM
FOURIER_GROUPS = 4
FOURIER_GDIM = D_MODEL // FOURIER_GROUPS
D_FF = 2816
N_MOD = 9
ALPHA = (2 * DEPTH) ** 0.25
BETA = (8 * DEPTH) ** -0.25
LN_EPS = 1e-5

kernel_name = "hybrid_pool_swa_fourier_macaron_dit"


def layer_norm(x, g=None, b=None):
    x32 = x.astype(jnp.float32)
    mu = jnp.mean(x32, axis=-1, keepdims=True)
    var = jnp.mean(jnp.square(x32 - mu), axis=-1, keepdims=True)
    y = (x32 - mu) * lax.rsqrt(var + LN_EPS)
    if g is not None:
        y = y * g.astype(jnp.float32) + b.astype(jnp.float32)
    return y.astype(x.dtype)


def modulate(x, shift, scale):
    return layer_norm(x) * (1 + scale) + shift


def swiglu(h, w1, w3, w2):
    return (jax.nn.silu(h @ w1) * (h @ w3)) @ w2


def rotate(t, ang):
    cos = jnp.cos(ang)[:, None, :].astype(t.dtype)
    sin = jnp.sin(ang)[:, None, :].astype(t.dtype)
    t1, t2 = jnp.split(t, 2, axis=-1)
    return jnp.concatenate([t1 * cos - t2 * sin, t1 * sin + t2 * cos], axis=-1)


def rope_2d(t, ang_row, ang_col):
    return jnp.concatenate([rotate(t[..., :ROPE_HALF], ang_row), rotate(t[..., ROPE_HALF:], ang_col)], axis=-1)


def pool_mixer(p, w_pool, pool_scale):
    b_, L, _ = p.shape
    p32 = p.astype(jnp.float32).reshape(b_, L, POOL_GROUPS, POOL_GDIM)
    csum = jnp.concatenate([jnp.zeros_like(p32[:, :1]), jnp.cumsum(p32, axis=1)], axis=1)
    t = jnp.arange(L)
    outs = []
    for g, w in enumerate(POOL_WINDOWS):
        lo = jnp.clip(t - w // 2, 0, L)
        hi = jnp.clip(t + w // 2, 0, L)
        cs = csum[:, :, g]
        win_sum = jnp.take(cs, hi, axis=1) - jnp.take(cs, lo, axis=1)
        mean = win_sum / (hi - lo).astype(jnp.float32)[None, :, None]
        outs.append((mean - p32[:, :, g]).astype(p.dtype) @ w_pool[g])
    return jnp.concatenate(outs, axis=-1) * pool_scale


def banded_gqa(q, k, v, k_ctx, v_ctx, sinks):
    b_, L = q.shape[:2]
    nb = L // BLOCK
    scale = HEAD_DIM ** -0.5
    qb = q.reshape(b_, nb, BLOCK, N_KV_HEADS, GQA_GROUP, HEAD_DIM)

    def band(t):
        tp = jnp.pad(t, ((0, 0), (BLOCK, BLOCK), (0, 0), (0, 0))).reshape(b_, nb + 2, BLOCK, N_KV_HEADS, HEAD_DIM)
        return jnp.concatenate([tp[:, :-2], tp[:, 1:-1], tp[:, 2:]], axis=2)

    kb, vb = band(k), band(v)
    s_loc = jnp.einsum('bnqhgd,bnkhd->bnhgqk', qb, kb).astype(jnp.float32) * scale
    s_ctx = jnp.einsum('bnqhgd,bchd->bnhgqc', qb, k_ctx).astype(jnp.float32) * scale
    qi = jnp.arange(BLOCK)[:, None]
    ki = jnp.arange(3 * BLOCK)[None, :]
    key_pos = jnp.arange(nb)[:, None, None] * BLOCK - BLOCK + ki[None]
    valid = (jnp.abs(ki - BLOCK - qi) <= WINDOW)[None] & (key_pos >= 0) & (key_pos < L)
    s_loc = jnp.where(valid[None, :, None, None], s_loc, -jnp.inf)
    sink = sinks.astype(jnp.float32).reshape(1, 1, N_KV_HEADS, GQA_GROUP, 1, 1)
    m = jnp.maximum(jnp.maximum(jnp.max(s_loc, axis=-1, keepdims=True), jnp.max(s_ctx, axis=-1, keepdims=True)), sink)
    e_loc = jnp.exp(s_loc - m)
    e_ctx = jnp.exp(s_ctx - m)
    denom = jnp.sum(e_loc, axis=-1, keepdims=True) + jnp.sum(e_ctx, axis=-1, keepdims=True) + jnp.exp(sink - m)
    out = (jnp.einsum('bnhgqk,bnkhd->bnhgqd', e_loc, vb.astype(jnp.float32))
           + jnp.einsum('bnhgqc,bchd->bnhgqd', e_ctx, v_ctx.astype(jnp.float32))) / denom
    out = jnp.transpose(out, (0, 1, 4, 2, 3, 5)).astype(q.dtype)
    return out.reshape(b_, L, Q_DIM)


def mixer_pool_attn(h, h_ctx, w_in, w_pool, pool_scale, sinks, w_out, ang_row, ang_col):
    b_, L, _ = h.shape
    u = h @ w_in
    p, q, k, v = jnp.split(u, [POOL_DIM, POOL_DIM + Q_DIM, POOL_DIM + Q_DIM + KV_DIM], axis=-1)
    kv_ctx = h_ctx @ w_in[:, POOL_DIM + Q_DIM:]
    k_ctx, v_ctx = jnp.split(kv_ctx, 2, axis=-1)
    n_ctx = h_ctx.shape[1]
    q = rope_2d(q.reshape(b_, L, N_Q_HEADS, HEAD_DIM), ang_row, ang_col)
    k = rope_2d(k.reshape(b_, L, N_KV_HEADS, HEAD_DIM), ang_row, ang_col)
    v = v.reshape(b_, L, N_KV_HEADS, HEAD_DIM)
    k_ctx = k_ctx.reshape(b_, n_ctx, N_KV_HEADS, HEAD_DIM)
    v_ctx = v_ctx.reshape(b_, n_ctx, N_KV_HEADS, HEAD_DIM)
    a = pool_mixer(p, w_pool, pool_scale)
    o = banded_gqa(q, k, v, k_ctx, v_ctx, sinks)
    return jnp.concatenate([a, o], axis=-1) @ w_out


def fourier_mixer(h, w_out):
    b_, L, _ = h.shape
    hg = h.astype(jnp.float32).reshape(b_, L, FOURIER_GROUPS, FOURIER_GDIM)
    f = jnp.fft.fft2(hg, axes=(1, 3), norm='ortho').real
    return f.reshape(b_, L, D_MODEL).astype(h.dtype) @ w_out


def setup_inputs(seed: int = 0) -> dict:
    key = jax.random.key(seed)
    ks = jax.random.split(key, 20)

    def nrm(k, shape, s):
        return jax.random.normal(k, shape, jnp.float32) * s

    return {
        'x': nrm(ks[0], (BATCH, SEQ, D_MODEL), 1.0),
        'c': nrm(ks[1], (BATCH, D_MODEL), 1.0),
        'ctx': nrm(ks[2], (BATCH, CTX_LEN, D_MODEL), 1.0),
        'c_ctx': nrm(ks[3], (D_MODEL,), 1.0),
        'ada_w': nrm(ks[4], (DEPTH, D_MODEL, N_MOD * D_MODEL), 0.5 * D_MODEL ** -0.5),
        'ada_b': nrm(ks[5], (DEPTH, N_MOD * D_MODEL), 0.02),
        'ln_g': 1.0 + nrm(ks[6], (DEPTH, 3, D_MODEL), 0.02),
        'ln_b': nrm(ks[7], (DEPTH, 3, D_MODEL), 0.02),
        'ffn1_w1': nrm(ks[8], (DEPTH, D_MODEL, D_FF), D_MODEL ** -0.5),
        'ffn1_w3': nrm(ks[9], (DEPTH, D_MODEL, D_FF), D_MODEL ** -0.5),
        'ffn1_w2': nrm(ks[10], (DEPTH, D_FF, D_MODEL), BETA * D_FF ** -0.5),
        'ffn2_w1': nrm(ks[11], (DEPTH, D_MODEL, D_FF), D_MODEL ** -0.5),
        'ffn2_w3': nrm(ks[12], (DEPTH, D_MODEL, D_FF), D_MODEL ** -0.5),
        'ffn2_w2': nrm(ks[13], (DEPTH, D_FF, D_MODEL), BETA * D_FF ** -0.5),
        'mix_w_in': nrm(ks[14], (N_EVEN, D_MODEL, IN_DIM), D_MODEL ** -0.5),
        'pool_w': nrm(ks[15], (N_EVEN, POOL_GROUPS, POOL_GDIM, POOL_GDIM), POOL_GDIM ** -0.5),
        'pool_scale': 1.0 + nrm(ks[16], (N_EVEN, POOL_DIM), 0.02),
        'attn_sinks': nrm(ks[17], (N_EVEN, N_Q_HEADS), 0.5),
        'mix_w_out': nrm(ks[18], (N_EVEN, MIX_DIM, D_MODEL), BETA * MIX_DIM ** -0.5),
        'fourier_w_out': nrm(ks[19], (N_ODD, D_MODEL, D_MODEL), BETA * D_MODEL ** -0.5),
    }


def reference(x, c, ctx, c_ctx, ada_w, ada_b, ln_g, ln_b, ffn1_w1, ffn1_w3, ffn1_w2, ffn2_w1, ffn2_w3, ffn2_w2,
              mix_w_in, pool_w, pool_scale, attn_sinks, mix_w_out, fourier_w_out):
    L = x.shape[1]
    ROWS = L // GRID_W
    row = jnp.repeat(jnp.arange(ROWS), GRID_W).astype(jnp.float32)
    col = jnp.tile(jnp.arange(GRID_W), ROWS).astype(jnp.float32)
    freqs = ROPE_BASE ** (-jnp.arange(0, ROPE_HALF, 2, dtype=jnp.float32) / ROPE_HALF)
    ang_row = row[:, None] * freqs[None]
    ang_col = col[:, None] * freqs[None]
    silu_c = jax.nn.silu(c)
    silu_cc = jax.nn.silu(c_ctx)
    for l in range(DEPTH):
        mod = (silu_c @ ada_w[l] + ada_b[l])[:, None, :]
        sh1, sc1, g1, sh2, sc2, g2, sh3, sc3, g3 = jnp.split(mod, N_MOD, axis=-1)
        y = swiglu(modulate(x, sh1, sc1), ffn1_w1[l], ffn1_w3[l], ffn1_w2[l])
        x = layer_norm(ALPHA * x + 0.5 * g1 * y, ln_g[l, 0], ln_b[l, 0])
        if l % 2 == 0:
            e = l // 2
            mod_c = (silu_cc @ ada_w[l][:, :5 * D_MODEL] + ada_b[l][:5 * D_MODEL])[None, None, :]
            csh1, csc1, cg1, csh2, csc2 = jnp.split(mod_c, 5, axis=-1)
            yc = swiglu(modulate(ctx, csh1, csc1), ffn1_w1[l], ffn1_w3[l], ffn1_w2[l])
            ctx = layer_norm(ALPHA * ctx + 0.5 * cg1 * yc, ln_g[l, 0], ln_b[l, 0])
            mix = mixer_pool_attn(modulate(x, sh2, sc2), modulate(ctx, csh2, csc2), mix_w_in[e], pool_w[e],
                                  pool_scale[e], attn_sinks[e], mix_w_out[e], ang_row, ang_col)
        else:
            mix = fourier_mixer(modulate(x, sh2, sc2), fourier_w_out[l // 2])
        x = layer_norm(ALPHA * x + g2 * mix, ln_g[l, 1], ln_b[l, 1])
        y = swiglu(modulate(x, sh3, sc3), ffn2_w1[l], ffn2_w3[l], ffn2_w2[l])
        x = layer_norm(ALPHA * x + 0.5 * g3 * y, ln_g[l, 2], ln_b[l, 2])
    return x
```

```python
import functools

import numpy as np
import jax
import jax.numpy as jnp
from jax import lax
from jax.experimental import pallas as pl
from jax.experimental.pallas import tpu as pltpu

F32 = jnp.float32
BF16 = jnp.bfloat16

D_MODEL = 1024
DEPTH = 2
GRID_W = 64
POOL_GROUPS = 4
POOL_WINDOWS = (2, 4, 8, 16)
POOL_DIM = D_MODEL // 2
POOL_GDIM = POOL_DIM // POOL_GROUPS
HEAD_DIM = 64
N_Q_HEADS = (D_MODEL // 2) // HEAD_DIM
N_KV_HEADS = 2
GQA_GROUP = N_Q_HEADS // N_KV_HEADS
Q_DIM = N_Q_HEADS * HEAD_DIM
KV_DIM = N_KV_HEADS * HEAD_DIM
WINDOW = 128
BLOCK = 128
ROPE_BASE = 10000.0
ROPE_HALF = HEAD_DIM // 2
FOURIER_GROUPS = 4
FOURIER_GDIM = D_MODEL // FOURIER_GROUPS
N_MOD = 9
ALPHA = (2 * DEPTH) ** 0.25
LN_EPS = 1e-5
ATTN_SCALE = HEAD_DIM ** -0.5

LANES = 128
BF16_SUBLANES = 16
V7X_VMEM_BYTES = 64 * 1024 * 1024

U_Q = N_Q_HEADS * LANES
U_P = POOL_DIM
U_KV = 4 * LANES
U_COLS = U_Q + U_P + U_KV
POOL_HALO = BF16_SUBLANES
NEG_BIG = -1e30


def _ln(x):
    mu = jnp.mean(x, axis=-1, keepdims=True)
    xc = x - mu
    var = jnp.mean(xc * xc, axis=-1, keepdims=True)
    return xc * lax.rsqrt(var + LN_EPS)


def _modulate_bf16(x, mod_ref):
    return (_ln(x) * (1.0 + mod_ref[1:2, :]) + mod_ref[0:1, :]).astype(BF16)


def _params(vmem_bytes):
    return pltpu.CompilerParams(dimension_semantics=("parallel",), vmem_limit_bytes=int(vmem_bytes))


def _params2(vmem_bytes):
    return pltpu.CompilerParams(dimension_semantics=("parallel", "parallel"), vmem_limit_bytes=int(vmem_bytes))


def _resident(shape):
    nd = len(shape)
    return pl.BlockSpec(shape, lambda *_: (0,) * nd, pipeline_mode=pl.Buffered(1))


def _ada_kernel(c_ref, w_ref, b_ref, o_ref):
    c = c_ref[...]
    s = c * jax.nn.sigmoid(c)
    o_ref[...] = jnp.dot(s, w_ref[...], preferred_element_type=F32) + b_ref[...]


def _ada(c_rows, ada_w, ada_b):
    r, d = c_rows.shape
    depth, _, n = ada_w.shape
    tn = n // 8
    return pl.pallas_call(
        _ada_kernel,
        grid=(depth, n // tn),
        in_specs=[
            pl.BlockSpec((r, d), lambda l, j: (0, 0)),
            pl.BlockSpec((None, d, tn), lambda l, j: (l, 0, j)),
            pl.BlockSpec((None, 1, tn), lambda l, j: (l, 0, j)),
        ],
        out_specs=pl.BlockSpec((None, r, tn), lambda l, j: (l, 0, j)),
        out_shape=jax.ShapeDtypeStruct((depth, r, n), F32),
        compiler_params=_params2(2 * d * tn * 4 + (4 << 20)),
        name="ada",
    )(c_rows, ada_w, ada_b.reshape(depth, 1, n))


def _ffn_kernel(x_ref, mod_ref, lng_ref, lnb_ref, w1_ref, w3_ref, w2_ref, o_ref):
    x = x_ref[...]
    h = _modulate_bf16(x, mod_ref)
    a = jnp.dot(h, w1_ref[...], preferred_element_type=F32)
    b = jnp.dot(h, w3_ref[...], preferred_element_type=F32)
    act = (a * jax.nn.sigmoid(a) * b).astype(BF16)
    y = jnp.dot(act, w2_ref[...], preferred_element_type=F32)
    z = ALPHA * x + (0.5 * mod_ref[2:3, :]) * y
    o_ref[...] = _ln(z) * lng_ref[...] + lnb_ref[...]


def _ffn(x, mod, lng, lnb, w1, w3, w2, *, seq_len, tm):
    n, d = x.shape
    f = w1.shape[1]
    tiles_per_seq = seq_len // tm
    mod_idx = (lambda i: (0, 0, 0)) if mod.shape[0] == 1 else (lambda i: (i // tiles_per_seq, 0, 0))
    vmem = 4 * tm * d * 4 + 3 * d * f * 2 + tm * f * 14 + tm * d * 12 + (4 << 20)
    return pl.pallas_call(
        _ffn_kernel,
        grid=(n // tm,),
        in_specs=[
            pl.BlockSpec((tm, d), lambda i: (i, 0)),
            pl.BlockSpec((None, 3, d), mod_idx),
            _resident((1, d)),
            _resident((1, d)),
            _resident((d, f)),
            _resident((d, f)),
            _resident((f, d)),
        ],
        out_specs=pl.BlockSpec((tm, d), lambda i: (i, 0)),
        out_shape=jax.ShapeDtypeStruct((n, d), F32),
        compiler_params=_params(vmem),
        name="ffn",
    )(x, mod, lng, lnb, w1, w3, w2)


def _proj_kernel(x_ref, mod_ref, w_ref, cos_ref, sa_ref, sb_ref, o_ref):
    tm = x_ref.shape[0]
    h = _modulate_bf16(x_ref[...], mod_ref)
    u = jnp.dot(h, w_ref[...], preferred_element_type=F32)
    cos, sa, sb = cos_ref[...], sa_ref[...], sb_ref[...]
    low = lax.broadcasted_iota(jnp.int32, (tm, LANES), 1) < HEAD_DIM

    def rope(t):
        return t * cos + pltpu.roll(t, LANES - ROPE_HALF // 2, 1) * sa + pltpu.roll(t, ROPE_HALF // 2, 1) * sb

    for c in range(Q_DIM // LANES):
        qc = rope(u[:, POOL_DIM + c * LANES:POOL_DIM + (c + 1) * LANES]) * ATTN_SCALE
        o_ref[:, (2 * c) * LANES:(2 * c + 1) * LANES] = jnp.where(low, qc, 0.0).astype(BF16)
        o_ref[:, (2 * c + 1) * LANES:(2 * c + 2) * LANES] = jnp.where(low, 0.0, qc).astype(BF16)
    o_ref[:, U_Q:U_Q + U_P] = u[:, :POOL_DIM].astype(BF16)
    k = rope(u[:, POOL_DIM + Q_DIM:POOL_DIM + Q_DIM + KV_DIM])
    v = u[:, POOL_DIM + Q_DIM + KV_DIM:]
    base = U_Q + U_P
    for j, t in enumerate((k, v)):
        ts = pltpu.roll(t, HEAD_DIM, 1)
        o_ref[:, base + (2 * j) * LANES:base + (2 * j + 1) * LANES] = jnp.where(low, t, ts).astype(BF16)
        o_ref[:, base + (2 * j + 1) * LANES:base + (2 * j + 2) * LANES] = jnp.where(low, ts, t).astype(BF16)


def _proj(x, mod, w_in, rope_tabs, *, seq_len, tm):
    n, d = x.shape
    tiles_per_seq = seq_len // tm
    mod_idx = (lambda i: (0, 0, 0)) if mod.shape[0] == 1 else (lambda i: (i // tiles_per_seq, 0, 0))
    tab_spec = pl.BlockSpec((tm, LANES), lambda i: (i % tiles_per_seq, 0))
    vmem = 2 * tm * d * 4 + 2 * tm * U_COLS * 2 + 2 * d * w_in.shape[1] * 2 + tm * w_in.shape[1] * 8 + (8 << 20)
    return pl.pallas_call(
        _proj_kernel,
        grid=(n // tm,),
        in_specs=[
            pl.BlockSpec((tm, d), lambda i: (i, 0)),
            pl.BlockSpec((None, 3, d), mod_idx),
            _resident(w_in.shape),
            tab_spec, tab_spec, tab_spec,
        ],
        out_specs=pl.BlockSpec((tm, U_COLS), lambda i: (i, 0)),
        out_shape=jax.ShapeDtypeStruct((n, U_COLS), BF16),
        compiler_params=_params(vmem),
        name="proj",
    )(x, mod, w_in, *rope_tabs)


def _rope_tables(seq_len):
    rows = seq_len // GRID_W
    row = jnp.repeat(jnp.arange(rows), GRID_W).astype(F32)
    col = jnp.tile(jnp.arange(GRID_W), rows).astype(F32)
    freqs = ROPE_BASE ** (-jnp.arange(0, ROPE_HALF, 2, dtype=F32) / ROPE_HALF)
    ang_row = row[:, None] * freqs[None]
    ang_col = col[:, None] * freqs[None]
    cr, sr, cc, sc = jnp.cos(ang_row), jnp.sin(ang_row), jnp.cos(ang_col), jnp.sin(ang_col)
    z = jnp.zeros_like(sr)
    cos = jnp.concatenate([cr, cr, cc, cc], axis=-1)
    sa = jnp.concatenate([-sr, z, -sc, z], axis=-1)
    sb = jnp.concatenate([z, sr, z, sc], axis=-1)
    return tuple(jnp.tile(t, (1, LANES // HEAD_DIM)) for t in (cos, sa, sb))


def _identity_rope_tables(seq_len):
    return (jnp.ones((seq_len, LANES), F32), jnp.zeros((seq_len, LANES), F32), jnp.zeros((seq_len, LANES), F32))


def _pool_band():
    r = np.arange(BLOCK)[:, None]
    j = np.arange(BLOCK + 2 * POOL_HALO)[None, :] - POOL_HALO
    return np.stack([((j >= r - w // 2) & (j < r + w // 2)) for w in POOL_WINDOWS]).astype(np.float32)


def _mix0_kernel(sinks_ref, x_ref, q_ref, pprev_ref, pcur_ref, pnext_ref, kvprev_ref, kvcur_ref, kvnext_ref,
                 kvctx_ref, mod_ref, band_ref, wpool_ref, pscale_ref, wout_ref, lng_ref, lnb_ref,
                 o_ref, cat_ref, *, tiles_per_seq, seq_len):
    tm = x_ref.shape[0]
    nb = tm // BLOCK
    assert nb >= 2
    n_ctx = kvctx_ref.shape[0]
    n_keys = 3 * BLOCK + n_ctx
    it = pl.program_id(0) % tiles_per_seq
    first = it == 0
    last = it == tiles_per_seq - 1

    halo_zero = jnp.zeros((POOL_HALO, POOL_DIM), BF16)
    pext = jnp.concatenate([jnp.where(first, halo_zero, pprev_ref[...]), pcur_ref[...],
                            jnp.where(last, halo_zero, pnext_ref[...])], axis=0)
    row = lax.broadcasted_iota(jnp.int32, (BLOCK, POOL_GDIM), 0)
    for s in range(nb):
        t = it * tm + s * BLOCK + row
        for g, w in enumerate(POOL_WINDOWS):
            cols = slice(g * POOL_GDIM, (g + 1) * POOL_GDIM)
            cnt = jnp.minimum(t + w // 2, seq_len) - jnp.maximum(t - w // 2, 0)
            win = jnp.dot(band_ref[g], pext[s * BLOCK:(s + 1) * BLOCK + 2 * POOL_HALO, cols],
                          preferred_element_type=F32)
            pc = pcur_ref[s * BLOCK:(s + 1) * BLOCK, cols].astype(F32)
            dlt = (win / cnt.astype(F32) - pc).astype(BF16)
            a = jnp.dot(dlt, wpool_ref[g], preferred_element_type=F32) * pscale_ref[:, cols]
            cat_ref[s * BLOCK:(s + 1) * BLOCK, cols] = a.astype(BF16)

    qi = lax.broadcasted_iota(jnp.int32, (GQA_GROUP * BLOCK, n_keys), 0) & (BLOCK - 1)
    ki = lax.broadcasted_iota(jnp.int32, (GQA_GROUP * BLOCK, n_keys), 1)
    in_band = ((ki >= qi) & (ki <= qi + 2 * WINDOW)) | (ki >= 3 * BLOCK)
    bias = jnp.where(in_band, 0.0, NEG_BIG)
    bias_first = jnp.where(first & (ki < BLOCK), NEG_BIG, bias)
    bias_last = jnp.where(last & (ki >= 2 * BLOCK) & (ki < 3 * BLOCK), NEG_BIG, bias)
    low = lax.broadcasted_iota(jnp.int32, (BLOCK, LANES), 1) < HEAD_DIM

    for h in range(N_KV_HEADS):
        kcol = slice(h * LANES, (h + 1) * LANES)
        vcol = slice((N_KV_HEADS + h) * LANES, (N_KV_HEADS + h + 1) * LANES)
        kext = jnp.concatenate([kvprev_ref[:, kcol], kvcur_ref[:, kcol], kvnext_ref[:, kcol]], axis=0)
        vext = jnp.concatenate([kvprev_ref[:, vcol], kvcur_ref[:, vcol], kvnext_ref[:, vcol]], axis=0)
        kctx = kvctx_ref[:, kcol]
        vctx = kvctx_ref[:, vcol]
        sink = jnp.concatenate([jnp.full((BLOCK, 1), sinks_ref[h * GQA_GROUP + g], F32) for g in range(GQA_GROUP)],
                               axis=0)
        for s in range(nb):
            rows = slice(s * BLOCK, (s + 1) * BLOCK)
            qst = jnp.concatenate([q_ref[rows, (h * GQA_GROUP + g) * LANES:(h * GQA_GROUP + g + 1) * LANES]
                                   for g in range(GQA_GROUP)], axis=0)
            kcat = jnp.concatenate([kext[s * BLOCK:(s + 3) * BLOCK], kctx], axis=0)
            vcat = jnp.concatenate([vext[s * BLOCK:(s + 3) * BLOCK], vctx], axis=0)
            b = bias_first if s == 0 else (bias_last if s == nb - 1 else bias)
            sc = lax.dot_general(qst, kcat, (((1,), (1,)), ((), ())), preferred_element_type=F32) + b
            m = jnp.maximum(jnp.max(sc, axis=-1, keepdims=True), sink)
            e = jnp.exp(sc - m)
            den = jnp.sum(e, axis=-1, keepdims=True) + jnp.exp(sink - m)
            o = jnp.dot(e.astype(BF16), vcat, preferred_element_type=F32) / den
            for j in range(GQA_GROUP // 2):
                pair = jnp.where(low, o[(2 * j) * BLOCK:(2 * j + 1) * BLOCK], o[(2 * j + 1) * BLOCK:(2 * j + 2) * BLOCK])
                c0 = POOL_DIM + (h * GQA_GROUP // 2 + j) * LANES
                cat_ref[rows, c0:c0 + LANES] = pair.astype(BF16)

    mix = jnp.dot(cat_ref[...], wout_ref[...], preferred_element_type=F32)
    z = ALPHA * x_ref[...] + mod_ref[2:3, :] * mix
    o_ref[...] = _ln(z) * lng_ref[...] + lnb_ref[...]


def _mix0(x, u, u_ctx, mod, sinks, w_pool, pool_scale, w_out, lng, lnb, *, seq_len, n_ctx, tm):
    n, d = x.shape
    nb = tm // BLOCK
    tiles_per_seq = seq_len // tm
    hb = tm // POOL_HALO
    q_blk, p_blk, kv_blk = 0, U_Q // U_P, (U_Q + U_P) // U_KV
    band = jnp.asarray(_pool_band(), BF16)
    kernel = functools.partial(_mix0_kernel, tiles_per_seq=tiles_per_seq, seq_len=seq_len)
    vmem = 4 * tm * d * 4 + 4 * tm * U_COLS * 2 + 4 * d * d * 2 + 6 * GQA_GROUP * BLOCK * (3 * BLOCK + n_ctx) * 4 + (8 << 20)
    return pl.pallas_call(
        kernel,
        grid=(n // tm,),
        in_specs=[
            pl.BlockSpec(memory_space=pltpu.SMEM),
            pl.BlockSpec((tm, d), lambda i: (i, 0)),
            pl.BlockSpec((tm, U_Q), lambda i: (i, q_blk)),
            pl.BlockSpec((POOL_HALO, U_P), lambda i: (jnp.maximum(i * hb - 1, 0), p_blk)),
            pl.BlockSpec((tm, U_P), lambda i: (i, p_blk)),
            pl.BlockSpec((POOL_HALO, U_P), lambda i: (jnp.minimum((i + 1) * hb, n // POOL_HALO - 1), p_blk)),
            pl.BlockSpec((BLOCK, U_KV), lambda i: (jnp.maximum(i * nb - 1, 0), kv_blk)),
            pl.BlockSpec((tm, U_KV), lambda i: (i, kv_blk)),
            pl.BlockSpec((BLOCK, U_KV), lambda i: (jnp.minimum((i + 1) * nb, n // BLOCK - 1), kv_blk)),
            pl.BlockSpec((n_ctx, U_KV), lambda i: (i // tiles_per_seq, kv_blk)),
            pl.BlockSpec((None, 3, d), lambda i: (i // tiles_per_seq, 0, 0)),
            _resident(band.shape),
            _resident(w_pool.shape),
            _resident((1, POOL_DIM)),
            _resident(w_out.shape),
            _resident((1, d)),
            _resident((1, d)),
        ],
        out_specs=pl.BlockSpec((tm, d), lambda i: (i, 0)),
        out_shape=jax.ShapeDtypeStruct((n, d), F32),
        scratch_shapes=[pltpu.VMEM((tm, POOL_DIM + Q_DIM), BF16)],
        compiler_params=_params(vmem),
        name="mix0",
    )(sinks, x, u, u, u, u, u, u, u, u_ctx, mod, band, w_pool, pool_scale, w_out, lng, lnb)


def _dft_tables(seq_len):
    t1n, t2n, c = seq_len // GRID_W, GRID_W, FOURIER_GDIM
    k1 = np.arange(t1n)
    a1 = 2 * np.pi * ((np.outer(k1, np.arange(t1n)) % t1n) / t1n)[None] \
        + 2 * np.pi * (np.outer(np.arange(t2n), k1) / seq_len)[:, :, None]
    m1 = np.concatenate([np.cos(a1), -np.sin(a1)], axis=1) / np.sqrt(t1n)
    a2 = 2 * np.pi * (np.outer(np.arange(t2n), np.arange(t2n)) % t2n) / t2n
    wr, wi = np.cos(a2) / np.sqrt(t2n), -np.sin(a2) / np.sqrt(t2n)
    m2 = np.block([[wr, -wi], [wi, wr]])
    a3 = 2 * np.pi * (np.outer(np.arange(c), np.arange(c)) % c) / c
    cs = np.concatenate([np.cos(a3), np.sin(a3)], axis=0) / np.sqrt(c)
    return tuple(jnp.asarray(t, F32).astype(BF16) for t in (m1, m2, cs))


def _dft1_kernel(x_ref, mod_ref, m1_ref, zr_ref, zi_ref):
    t1n = x_ref.shape[0]
    d = mod_ref.shape[1]
    for j in range(m1_ref.shape[0]):
        cols = slice(j * d, (j + 1) * d)
        h = _modulate_bf16(x_ref[:, cols], mod_ref)
        z = jnp.dot(m1_ref[j], h, preferred_element_type=F32)
        zr_ref[:, cols] = z[:t1n].astype(BF16)
        zi_ref[:, cols] = z[t1n:].astype(BF16)


def _dft2_kernel(zr_ref, zi_ref, m2_ref, yr_ref, yi_ref):
    t2n = zr_ref.shape[1]
    for k in range(zr_ref.shape[0]):
        zc = jnp.concatenate([zr_ref[k], zi_ref[k]], axis=0)
        y = jnp.dot(m2_ref[...], zc, preferred_element_type=F32)
        yr_ref[k] = y[:t2n].astype(BF16)
        yi_ref[k] = y[t2n:].astype(BF16)


def _dft3_kernel(x_ref, yr_ref, yi_ref, mod_ref, cs_ref, wout_ref, lng_ref, lnb_ref, o_ref):
    d = mod_ref.shape[1]
    nk2 = yr_ref.shape[1] // d
    yr = jnp.concatenate([yr_ref[:, j * d:(j + 1) * d] for j in range(nk2)], axis=0)
    yi = jnp.concatenate([yi_ref[:, j * d:(j + 1) * d] for j in range(nk2)], axis=0)
    parts = []
    for g in range(FOURIER_GROUPS):
        cols = slice(g * FOURIER_GDIM, (g + 1) * FOURIER_GDIM)
        ycat = jnp.concatenate([yr[:, cols], yi[:, cols]], axis=1)
        parts.append(jnp.dot(ycat, cs_ref[...], preferred_element_type=F32))
    f = jnp.concatenate(parts, axis=1).astype(BF16)
    mix = jnp.dot(f, wout_ref[...], preferred_element_type=F32)
    z = ALPHA * x_ref[...] + mod_ref[2:3, :] * mix
    o_ref[...] = _ln(z) * lng_ref[...] + lnb_ref[...]


def _fourier_mix(x, mod, w_out, lng, lnb, *, batch, seq_len, nt2=4, nk1=8, nk2=4):
    n, d = x.shape
    t1n, t2n = seq_len // GRID_W, GRID_W
    m1, m2, cs = _dft_tables(seq_len)
    wide = jax.ShapeDtypeStruct((batch, t1n, t2n * d), BF16)
    zr, zi = pl.pallas_call(
        _dft1_kernel,
        grid=(batch, t2n // nt2),
        in_specs=[
            pl.BlockSpec((None, t1n, nt2 * d), lambda b, j: (b, 0, j)),
            pl.BlockSpec((None, 3, d), lambda b, j: (b, 0, 0)),
            pl.BlockSpec((nt2, 2 * t1n, t1n), lambda b, j: (j, 0, 0)),
        ],
        out_specs=[pl.BlockSpec((None, t1n, nt2 * d), lambda b, j: (b, 0, j))] * 2,
        out_shape=[wide, wide],
        compiler_params=_params2(2 * t1n * nt2 * d * 8 + 8 * t1n * d * 4 + (8 << 20)),
        name="dft1",
    )(x.reshape(batch, t1n, t2n * d), mod, m1)

    cube = jax.ShapeDtypeStruct((batch, t1n, t2n, d), BF16)
    blk = pl.BlockSpec((None, nk1, t2n, d), lambda b, j: (b, j, 0, 0))
    yr, yi = pl.pallas_call(
        _dft2_kernel,
        grid=(batch, t1n // nk1),
        in_specs=[blk, blk, pl.BlockSpec(m2.shape, lambda b, j: (0, 0))],
        out_specs=[blk, blk],
        out_shape=[cube, cube],
        compiler_params=_params2(8 * nk1 * t2n * d * 2 + 8 * t2n * d * 4 + (8 << 20)),
        name="dft2",
    )(zr.reshape(batch, t1n, t2n, d), zi.reshape(batch, t1n, t2n, d), m2)

    tm = nk2 * t1n
    yblk = pl.BlockSpec((None, t1n, nk2 * d), lambda b, j: (b, 0, j))
    out = pl.pallas_call(
        _dft3_kernel,
        grid=(batch, t2n // nk2),
        in_specs=[
            pl.BlockSpec((None, tm, d), lambda b, j: (b, j, 0)),
            yblk, yblk,
            pl.BlockSpec((None, 3, d), lambda b, j: (b, 0, 0)),
            _resident(cs.shape),
            _resident(w_out.shape),
            _resident((1, d)),
            _resident((1, d)),
        ],
        out_specs=pl.BlockSpec((None, tm, d), lambda b, j: (b, j, 0)),
        out_shape=jax.ShapeDtypeStruct((batch, seq_len, d), F32),
        compiler_params=_params2(4 * tm * d * 4 + 4 * tm * d * 2 + 2 * d * d * 2 + 6 * tm * d * 4 + (8 << 20)),
        name="dft3",
    )(x.reshape(batch, seq_len, d), yr.reshape(batch, t1n, t2n * d), yi.reshape(batch, t1n, t2n * d),
      mod, cs, w_out, lng, lnb)
    return out.reshape(n, d)


FFN_TM = 256
PROJ_TM = 512
MIX_TM = 512


def kernel(x, c, ctx, c_ctx, ada_w, ada_b, ln_g, ln_b, ffn1_w1, ffn1_w3, ffn1_w2, ffn2_w1, ffn2_w3, ffn2_w2,
           mix_w_in, pool_w, pool_scale, attn_sinks, mix_w_out, fourier_w_out):
    batch, seq_len, d = x.shape
    n_ctx = ctx.shape[1]
    assert d == D_MODEL and seq_len % MIX_TM == 0 and seq_len % GRID_W == 0 and n_ctx % BLOCK == 0

    c_rows = jnp.zeros((2 * 8, d), F32).at[:batch].set(c).at[batch].set(c_ctx)
    mod_all = _ada(c_rows, ada_w, ada_b).reshape(DEPTH, 2 * 8, N_MOD, d)

    xf = x.reshape(batch * seq_len, d)
    cf = ctx.reshape(batch * n_ctx, d)
    bf = lambda w: w.astype(BF16)
    for l in range(DEPTH):
        mod = mod_all[l, :batch]
        lng = ln_g[l][:, None, :]
        lnb = ln_b[l][:, None, :]
        ffn1 = (bf(ffn1_w1[l]), bf(ffn1_w3[l]), bf(ffn1_w2[l]))
        xf = _ffn(xf, mod[:, 0:3], lng[0], lnb[0], *ffn1, seq_len=seq_len, tm=FFN_TM)
        if l % 2 == 0:
            e = l // 2
            mod_c = mod_all[l, batch:batch + 1]
            cf = _ffn(cf, mod_c[:, 0:3], lng[0], lnb[0], *ffn1, seq_len=n_ctx, tm=n_ctx)
            w_in = bf(mix_w_in[e])
            u = _proj(xf, mod[:, 3:6], w_in, _rope_tables(seq_len), seq_len=seq_len, tm=PROJ_TM)
            u_ctx = _proj(cf, mod_c[:, 3:6], w_in, _identity_rope_tables(n_ctx), seq_len=n_ctx, tm=n_ctx)
            xf = _mix0(xf, u, u_ctx, mod[:, 3:6], attn_sinks[e], bf(pool_w[e]), pool_scale[e][None, :],
                       bf(mix_w_out[e]), lng[1], lnb[1], seq_len=seq_len, n_ctx=n_ctx, tm=MIX_TM)
        else:
            xf = _fourier_mix(xf, mod[:, 3:6], bf(fourier_w_out[l // 2]), lng[1], lnb[1],
                              batch=batch, seq_len=seq_len)
        ffn2 = (bf(ffn2_w1[l]), bf(ffn2_w3[l]), bf(ffn2_w2[l]))
        xf = _ffn(xf, mod[:, 6:9], lng[2], lnb[2], *ffn2, seq_len=seq_len, tm=FFN_TM)
    return xf.reshape(batch, seq_len, d)
```

```python
import functools

import numpy as np
import jax
import jax.numpy as jnp
from jax import lax
from jax.experimental import pallas as pl
from jax.experimental.pallas import tpu as pltpu

F32 = jnp.float32
BF16 = jnp.bfloat16

D_MODEL = 1024
DEPTH = 2
GRID_W = 64
POOL_GROUPS = 4
POOL_WINDOWS = (2, 4, 8, 16)
POOL_DIM = D_MODEL // 2
POOL_GDIM = POOL_DIM // POOL_GROUPS
HEAD_DIM = 64
N_Q_HEADS = (D_MODEL // 2) // HEAD_DIM
N_KV_HEADS = 2
GQA_GROUP = N_Q_HEADS // N_KV_HEADS
Q_DIM = N_Q_HEADS * HEAD_DIM
KV_DIM = N_KV_HEADS * HEAD_DIM
WINDOW = 128
BLOCK = 128
ROPE_BASE = 10000.0
ROPE_HALF = HEAD_DIM // 2
FOURIER_GROUPS = 4
FOURIER_GDIM = D_MODEL // FOURIER_GROUPS
N_MOD = 9
ALPHA = (2 * DEPTH) ** 0.25
LN_EPS = 1e-5
ATTN_SCALE = HEAD_DIM ** -0.5

LANES = 128
BF16_SUBLANES = 16
V7X_VMEM_BYTES = 64 * 1024 * 1024

U_Q = N_Q_HEADS * LANES
U_P = POOL_DIM
U_KV = 4 * LANES
U_COLS = U_Q + U_P + U_KV
POOL_HALO = BF16_SUBLANES
NEG_BIG = -1e30


def _ln(x):
    mu = jnp.mean(x, axis=-1, keepdims=True)
    xc = x - mu
    var = jnp.mean(xc * xc, axis=-1, keepdims=True)
    return xc * lax.rsqrt(var + LN_EPS)


def _modulate_bf16(x, mod_ref):
    return (_ln(x) * (1.0 + mod_ref[1:2, :]) + mod_ref[0:1, :]).astype(BF16)


def _params(vmem_bytes):
    return pltpu.CompilerParams(dimension_semantics=("parallel",), vmem_limit_bytes=int(vmem_bytes))


def _params2(vmem_bytes):
    return pltpu.CompilerParams(dimension_semantics=("parallel", "parallel"), vmem_limit_bytes=int(vmem_bytes))


def _resident(shape):
    nd = len(shape)
    return pl.BlockSpec(shape, lambda *_: (0,) * nd, pipeline_mode=pl.Buffered(1))


def _ada_kernel(c_ref, w_ref, b_ref, o_ref):
    c = c_ref[...]
    s = c * jax.nn.sigmoid(c)
    o_ref[...] = jnp.dot(s, w_ref[...], preferred_element_type=F32) + b_ref[...]


def _ada(c_rows, ada_w, ada_b):
    r, d = c_rows.shape
    depth, _, n = ada_w.shape
    tn = n // 8
    return pl.pallas_call(
        _ada_kernel,
        grid=(depth, n // tn),
        in_specs=[
            pl.BlockSpec((r, d), lambda l, j: (0, 0)),
            pl.BlockSpec((None, d, tn), lambda l, j: (l, 0, j)),
            pl.BlockSpec((None, 1, tn), lambda l, j: (l, 0, j)),
        ],
        out_specs=pl.BlockSpec((None, r, tn), lambda l, j: (l, 0, j)),
        out_shape=jax.ShapeDtypeStruct((depth, r, n), F32),
        compiler_params=_params2(2 * d * tn * 4 + (4 << 20)),
        name="ada",
    )(c_rows, ada_w, ada_b.reshape(depth, 1, n))


def _ffn_kernel(x_ref, mod_ref, lng_ref, lnb_ref, w1_ref, w3_ref, w2_ref, o_ref, *, sub):
    for r in range(x_ref.shape[0] // sub):
        rows = slice(r * sub, (r + 1) * sub)
        x = x_ref[rows, :]
        h = _modulate_bf16(x, mod_ref)
        a = jnp.dot(h, w1_ref[...], preferred_element_type=F32)
        b = jnp.dot(h, w3_ref[...], preferred_element_type=F32)
        act = (a * jax.nn.sigmoid(a) * b).astype(BF16)
        y = jnp.dot(act, w2_ref[...], preferred_element_type=F32)
        z = ALPHA * x + (0.5 * mod_ref[2:3, :]) * y
        o_ref[rows, :] = _ln(z) * lng_ref[...] + lnb_ref[...]


def _ffn(x, mod, lng, lnb, w1, w3, w2, *, seq_len, tm, sub):
    n, d = x.shape
    f = w1.shape[1]
    tiles_per_seq = seq_len // tm
    mod_idx = (lambda i: (0, 0, 0)) if mod.shape[0] == 1 else (lambda i: (i // tiles_per_seq, 0, 0))
    vmem = 4 * tm * d * 4 + 3 * d * f * 2 + 2 * sub * f * 10 + (4 << 20)
    return pl.pallas_call(
        functools.partial(_ffn_kernel, sub=sub),
        grid=(n // tm,),
        in_specs=[
            pl.BlockSpec((tm, d), lambda i: (i, 0)),
            pl.BlockSpec((None, 3, d), mod_idx),
            _resident((1, d)),
            _resident((1, d)),
            _resident((d, f)),
            _resident((d, f)),
            _resident((f, d)),
        ],
        out_specs=pl.BlockSpec((tm, d), lambda i: (i, 0)),
        out_shape=jax.ShapeDtypeStruct((n, d), F32),
        compiler_params=_params(vmem),
        name="ffn",
    )(x, mod, lng, lnb, w1, w3, w2)


def _proj_kernel(x_ref, mod_ref, w_ref, cos_ref, sa_ref, sb_ref, o_ref):
    tm = x_ref.shape[0]
    h = _modulate_bf16(x_ref[...], mod_ref)
    u = jnp.dot(h, w_ref[...], preferred_element_type=F32)
    cos, sa, sb = cos_ref[...], sa_ref[...], sb_ref[...]
    low = lax.broadcasted_iota(jnp.int32, (tm, LANES), 1) < HEAD_DIM

    def rope(t):
        return t * cos + pltpu.roll(t, LANES - ROPE_HALF // 2, 1) * sa + pltpu.roll(t, ROPE_HALF // 2, 1) * sb

    for c in range(Q_DIM // LANES):
        qc = rope(u[:, POOL_DIM + c * LANES:POOL_DIM + (c + 1) * LANES]) * ATTN_SCALE
        o_ref[:, (2 * c) * LANES:(2 * c + 1) * LANES] = jnp.where(low, qc, 0.0).astype(BF16)
        o_ref[:, (2 * c + 1) * LANES:(2 * c + 2) * LANES] = jnp.where(low, 0.0, qc).astype(BF16)
    o_ref[:, U_Q:U_Q + U_P] = u[:, :POOL_DIM].astype(BF16)
    k = rope(u[:, POOL_DIM + Q_DIM:POOL_DIM + Q_DIM + KV_DIM])
    v = u[:, POOL_DIM + Q_DIM + KV_DIM:]
    base = U_Q + U_P
    for j, t in enumerate((k, v)):
        ts = pltpu.roll(t, HEAD_DIM, 1)
        o_ref[:, base + (2 * j) * LANES:base + (2 * j + 1) * LANES] = jnp.where(low, t, ts).astype(BF16)
        o_ref[:, base + (2 * j + 1) * LANES:base + (2 * j + 2) * LANES] = jnp.where(low, ts, t).astype(BF16)


def _proj(x, mod, w_in, rope_tabs, *, seq_len, tm):
    n, d = x.shape
    tiles_per_seq = seq_len // tm
    mod_idx = (lambda i: (0, 0, 0)) if mod.shape[0] == 1 else (lambda i: (i // tiles_per_seq, 0, 0))
    tab_spec = pl.BlockSpec((tm, LANES), lambda i: (i % tiles_per_seq, 0))
    vmem = 2 * tm * d * 4 + 2 * tm * U_COLS * 2 + 2 * d * w_in.shape[1] * 2 + tm * w_in.shape[1] * 8 + (8 << 20)
    return pl.pallas_call(
        _proj_kernel,
        grid=(n // tm,),
        in_specs=[
            pl.BlockSpec((tm, d), lambda i: (i, 0)),
            pl.BlockSpec((None, 3, d), mod_idx),
            _resident(w_in.shape),
            tab_spec, tab_spec, tab_spec,
        ],
        out_specs=pl.BlockSpec((tm, U_COLS), lambda i: (i, 0)),
        out_shape=jax.ShapeDtypeStruct((n, U_COLS), BF16),
        compiler_params=_params(vmem),
        name="proj",
    )(x, mod, w_in, *rope_tabs)


def _rope_tables(seq_len):
    rows = seq_len // GRID_W
    row = jnp.repeat(jnp.arange(rows), GRID_W).astype(F32)
    col = jnp.tile(jnp.arange(GRID_W), rows).astype(F32)
    freqs = ROPE_BASE ** (-jnp.arange(0, ROPE_HALF, 2, dtype=F32) / ROPE_HALF)
    ang_row = row[:, None] * freqs[None]
    ang_col = col[:, None] * freqs[None]
    cr, sr, cc, sc = jnp.cos(ang_row), jnp.sin(ang_row), jnp.cos(ang_col), jnp.sin(ang_col)
    z = jnp.zeros_like(sr)
    cos = jnp.concatenate([cr, cr, cc, cc], axis=-1)
    sa = jnp.concatenate([-sr, z, -sc, z], axis=-1)
    sb = jnp.concatenate([z, sr, z, sc], axis=-1)
    return tuple(jnp.tile(t, (1, LANES // HEAD_DIM)) for t in (cos, sa, sb))


def _identity_rope_tables(seq_len):
    return (jnp.ones((seq_len, LANES), F32), jnp.zeros((seq_len, LANES), F32), jnp.zeros((seq_len, LANES), F32))


def _pool_band():
    r = np.arange(BLOCK)[:, None]
    j = np.arange(BLOCK + 2 * POOL_HALO)[None, :] - POOL_HALO
    return np.stack([((j >= r - w // 2) & (j < r + w // 2)) for w in POOL_WINDOWS]).astype(np.float32)


def _mix0_kernel(sinks_ref, x_ref, q_ref, pprev_ref, pcur_ref, pnext_ref, kvprev_ref, kvcur_ref, kvnext_ref,
                 kvctx_ref, mod_ref, band_ref, wpool_ref, pscale_ref, wout_ref, lng_ref, lnb_ref,
                 o_ref, cat_ref, *, tiles_per_seq, seq_len):
    tm = x_ref.shape[0]
    nb = tm // BLOCK
    assert nb >= 2
    n_ctx = kvctx_ref.shape[0]
    n_keys = 3 * BLOCK + n_ctx
    it = pl.program_id(0) % tiles_per_seq
    first = it == 0
    last = it == tiles_per_seq - 1

    halo_zero = jnp.zeros((POOL_HALO, POOL_DIM), BF16)
    pext = jnp.concatenate([jnp.where(first, halo_zero, pprev_ref[...]), pcur_ref[...],
                            jnp.where(last, halo_zero, pnext_ref[...])], axis=0)
    row = lax.broadcasted_iota(jnp.int32, (BLOCK, POOL_GDIM), 0)
    for s in range(nb):
        t = it * tm + s * BLOCK + row
        for g, w in enumerate(POOL_WINDOWS):
            cols = slice(g * POOL_GDIM, (g + 1) * POOL_GDIM)
            cnt = jnp.minimum(t + w // 2, seq_len) - jnp.maximum(t - w // 2, 0)
            win = jnp.dot(band_ref[g], pext[s * BLOCK:(s + 1) * BLOCK + 2 * POOL_HALO, cols],
                          preferred_element_type=F32)
            pc = pcur_ref[s * BLOCK:(s + 1) * BLOCK, cols].astype(F32)
            dlt = (win / cnt.astype(F32) - pc).astype(BF16)
            a = jnp.dot(dlt, wpool_ref[g], preferred_element_type=F32) * pscale_ref[:, cols]
            cat_ref[s * BLOCK:(s + 1) * BLOCK, cols] = a.astype(BF16)

    qi = lax.broadcasted_iota(jnp.int32, (GQA_GROUP * BLOCK, n_keys), 0) & (BLOCK - 1)
    ki = lax.broadcasted_iota(jnp.int32, (GQA_GROUP * BLOCK, n_keys), 1)
    in_band = ((ki >= qi) & (ki <= qi + 2 * WINDOW)) | (ki >= 3 * BLOCK)
    bias = jnp.where(in_band, 0.0, NEG_BIG)
    bias_first = jnp.where(first & (ki < BLOCK), NEG_BIG, bias)
    bias_last = jnp.where(last & (ki >= 2 * BLOCK) & (ki < 3 * BLOCK), NEG_BIG, bias)
    low = lax.broadcasted_iota(jnp.int32, (BLOCK, LANES), 1) < HEAD_DIM

    for h in range(N_KV_HEADS):
        kcol = slice(h * LANES, (h + 1) * LANES)
        vcol = slice((N_KV_HEADS + h) * LANES, (N_KV_HEADS + h + 1) * LANES)
        kext = jnp.concatenate([kvprev_ref[:, kcol], kvcur_ref[:, kcol], kvnext_ref[:, kcol]], axis=0)
        vext = jnp.concatenate([kvprev_ref[:, vcol], kvcur_ref[:, vcol], kvnext_ref[:, vcol]], axis=0)
        kctx = kvctx_ref[:, kcol]
        vctx = kvctx_ref[:, vcol]
        sink = jnp.concatenate([jnp.full((BLOCK, 1), sinks_ref[h * GQA_GROUP + g], F32) for g in range(GQA_GROUP)],
                               axis=0)
        for s in range(nb):
            rows = slice(s * BLOCK, (s + 1) * BLOCK)
            qst = jnp.concatenate([q_ref[rows, (h * GQA_GROUP + g) * LANES:(h * GQA_GROUP + g + 1) * LANES]
                                   for g in range(GQA_GROUP)], axis=0)
            kcat = jnp.concatenate([kext[s * BLOCK:(s + 3) * BLOCK], kctx], axis=0)
            vcat = jnp.concatenate([vext[s * BLOCK:(s + 3) * BLOCK], vctx], axis=0)
            b = bias_first if s == 0 else (bias_last if s == nb - 1 else bias)
            sc = lax.dot_general(qst, kcat, (((1,), (1,)), ((), ())), preferred_element_type=F32) + b
            m = jnp.maximum(jnp.max(sc, axis=-1, keepdims=True), sink)
            e = jnp.exp(sc - m)
            den = jnp.sum(e, axis=-1, keepdims=True) + jnp.exp(sink - m)
            o = jnp.dot(e.astype(BF16), vcat, preferred_element_type=F32) / den
            for j in range(GQA_GROUP // 2):
                pair = jnp.where(low, o[(2 * j) * BLOCK:(2 * j + 1) * BLOCK], o[(2 * j + 1) * BLOCK:(2 * j + 2) * BLOCK])
                c0 = POOL_DIM + (h * GQA_GROUP // 2 + j) * LANES
                cat_ref[rows, c0:c0 + LANES] = pair.astype(BF16)

    mix = jnp.dot(cat_ref[...], wout_ref[...], preferred_element_type=F32)
    z = ALPHA * x_ref[...] + mod_ref[2:3, :] * mix
    o_ref[...] = _ln(z) * lng_ref[...] + lnb_ref[...]


def _mix0(x, u, u_ctx, mod, sinks, w_pool, pool_scale, w_out, lng, lnb, *, seq_len, n_ctx, tm):
    n, d = x.shape
    nb = tm // BLOCK
    tiles_per_seq = seq_len // tm
    hb = tm // POOL_HALO
    q_blk, p_blk, kv_blk = 0, U_Q // U_P, (U_Q + U_P) // U_KV
    band = jnp.asarray(_pool_band(), BF16)
    kernel = functools.partial(_mix0_kernel, tiles_per_seq=tiles_per_seq, seq_len=seq_len)
    vmem = 4 * tm * d * 4 + 4 * tm * U_COLS * 2 + 4 * d * d * 2 + 6 * GQA_GROUP * BLOCK * (3 * BLOCK + n_ctx) * 4 + (8 << 20)
    return pl.pallas_call(
        kernel,
        grid=(n // tm,),
        in_specs=[
            pl.BlockSpec(memory_space=pltpu.SMEM),
            pl.BlockSpec((tm, d), lambda i: (i, 0)),
            pl.BlockSpec((tm, U_Q), lambda i: (i, q_blk)),
            pl.BlockSpec((POOL_HALO, U_P), lambda i: (jnp.maximum(i * hb - 1, 0), p_blk)),
            pl.BlockSpec((tm, U_P), lambda i: (i, p_blk)),
            pl.BlockSpec((POOL_HALO, U_P), lambda i: (jnp.minimum((i + 1) * hb, n // POOL_HALO - 1), p_blk)),
            pl.BlockSpec((BLOCK, U_KV), lambda i: (jnp.maximum(i * nb - 1, 0), kv_blk)),
            pl.BlockSpec((tm, U_KV), lambda i: (i, kv_blk)),
            pl.BlockSpec((BLOCK, U_KV), lambda i: (jnp.minimum((i + 1) * nb, n // BLOCK - 1), kv_blk)),
            pl.BlockSpec((n_ctx, U_KV), lambda i: (i // tiles_per_seq, kv_blk)),
            pl.BlockSpec((None, 3, d), lambda i: (i // tiles_per_seq, 0, 0)),
            _resident(band.shape),
            _resident(w_pool.shape),
            _resident((1, POOL_DIM)),
            _resident(w_out.shape),
            _resident((1, d)),
            _resident((1, d)),
        ],
        out_specs=pl.BlockSpec((tm, d), lambda i: (i, 0)),
        out_shape=jax.ShapeDtypeStruct((n, d), F32),
        scratch_shapes=[pltpu.VMEM((tm, POOL_DIM + Q_DIM), BF16)],
        compiler_params=_params(vmem),
        name="mix0",
    )(sinks, x, u, u, u, u, u, u, u, u_ctx, mod, band, w_pool, pool_scale, w_out, lng, lnb)


def _dft_tables(seq_len):
    t1n, t2n, c = seq_len // GRID_W, GRID_W, FOURIER_GDIM
    k1 = np.arange(t1n)
    a1 = 2 * np.pi * ((np.outer(k1, np.arange(t1n)) % t1n) / t1n)[None] \
        + 2 * np.pi * (np.outer(np.arange(t2n), k1) / seq_len)[:, :, None]
    m1 = np.concatenate([np.cos(a1), -np.sin(a1)], axis=1) / np.sqrt(t1n)
    a2 = 2 * np.pi * (np.outer(np.arange(t2n), np.arange(t2n)) % t2n) / t2n
    wr, wi = np.cos(a2) / np.sqrt(t2n), -np.sin(a2) / np.sqrt(t2n)
    m2 = np.block([[wr, -wi], [wi, wr]])
    a3 = 2 * np.pi * (np.outer(np.arange(c), np.arange(c)) % c) / c
    cs = np.concatenate([np.cos(a3), np.sin(a3)], axis=0) / np.sqrt(c)
    return tuple(jnp.asarray(t, F32).astype(BF16) for t in (m1, m2, cs))


def _dft1_kernel(x_ref, mod_ref, m1_ref, zr_ref, zi_ref):
    t1n = x_ref.shape[0]
    d = mod_ref.shape[1]
    for j in range(m1_ref.shape[0]):
        cols = slice(j * d, (j + 1) * d)
        h = _modulate_bf16(x_ref[:, cols], mod_ref)
        z = jnp.dot(m1_ref[j], h, preferred_element_type=F32)
        zr_ref[:, cols] = z[:t1n].astype(BF16)
        zi_ref[:, cols] = z[t1n:].astype(BF16)


def _dft2_kernel(zr_ref, zi_ref, m2_ref, yr_ref, yi_ref):
    t2n = zr_ref.shape[1]
    for k in range(zr_ref.shape[0]):
        zc = jnp.concatenate([zr_ref[k], zi_ref[k]], axis=0)
        y = jnp.dot(m2_ref[...], zc, preferred_element_type=F32)
        yr_ref[k] = y[:t2n].astype(BF16)
        yi_ref[k] = y[t2n:].astype(BF16)


def _dft3_kernel(x_ref, yr_ref, yi_ref, mod_ref, cs_ref, wout_ref, lng_ref, lnb_ref, o_ref):
    d = mod_ref.shape[1]
    nk2 = yr_ref.shape[1] // d
    yr = jnp.concatenate([yr_ref[:, j * d:(j + 1) * d] for j in range(nk2)], axis=0)
    yi = jnp.concatenate([yi_ref[:, j * d:(j + 1) * d] for j in range(nk2)], axis=0)
    parts = []
    for g in range(FOURIER_GROUPS):
        cols = slice(g * FOURIER_GDIM, (g + 1) * FOURIER_GDIM)
        ycat = jnp.concatenate([yr[:, cols], yi[:, cols]], axis=1)
        parts.append(jnp.dot(ycat, cs_ref[...], preferred_element_type=F32))
    f = jnp.concatenate(parts, axis=1).astype(BF16)
    mix = jnp.dot(f, wout_ref[...], preferred_element_type=F32)
    z = ALPHA * x_ref[...] + mod_ref[2:3, :] * mix
    o_ref[...] = _ln(z) * lng_ref[...] + lnb_ref[...]


def _fourier_mix(x, mod, w_out, lng, lnb, *, batch, seq_len, nt2=4, nk1=8, nk2=4):
    n, d = x.shape
    t1n, t2n = seq_len // GRID_W, GRID_W
    m1, m2, cs = _dft_tables(seq_len)
    wide = jax.ShapeDtypeStruct((batch, t1n, t2n * d), BF16)
    zr, zi = pl.pallas_call(
        _dft1_kernel,
        grid=(batch, t2n // nt2),
        in_specs=[
            pl.BlockSpec((None, t1n, nt2 * d), lambda b, j: (b, 0, j)),
            pl.BlockSpec((None, 3, d), lambda b, j: (b, 0, 0)),
            pl.BlockSpec((nt2, 2 * t1n, t1n), lambda b, j: (j, 0, 0)),
        ],
        out_specs=[pl.BlockSpec((None, t1n, nt2 * d), lambda b, j: (b, 0, j))] * 2,
        out_shape=[wide, wide],
        compiler_params=_params2(2 * t1n * nt2 * d * 8 + 8 * t1n * d * 4 + (8 << 20)),
        name="dft1",
    )(x.reshape(batch, t1n, t2n * d), mod, m1)

    cube = jax.ShapeDtypeStruct((batch, t1n, t2n, d), BF16)
    blk = pl.BlockSpec((None, nk1, t2n, d), lambda b, j: (b, j, 0, 0))
    yr, yi = pl.pallas_call(
        _dft2_kernel,
        grid=(batch, t1n // nk1),
        in_specs=[blk, blk, pl.BlockSpec(m2.shape, lambda b, j: (0, 0))],
        out_specs=[blk, blk],
        out_shape=[cube, cube],
        compiler_params=_params2(8 * nk1 * t2n * d * 2 + 8 * t2n * d * 4 + (8 << 20)),
        name="dft2",
    )(zr.reshape(batch, t1n, t2n, d), zi.reshape(batch, t1n, t2n, d), m2)

    tm = nk2 * t1n
    yblk = pl.BlockSpec((None, t1n, nk2 * d), lambda b, j: (b, 0, j))
    out = pl.pallas_call(
        _dft3_kernel,
        grid=(batch, t2n // nk2),
        in_specs=[
            pl.BlockSpec((None, tm, d), lambda b, j: (b, j, 0)),
            yblk, yblk,
            pl.BlockSpec((None, 3, d), lambda b, j: (b, 0, 0)),
            _resident(cs.shape),
            _resident(w_out.shape),
            _resident((1, d)),
            _resident((1, d)),
        ],
        out_specs=pl.BlockSpec((None, tm, d), lambda b, j: (b, j, 0)),
        out_shape=jax.ShapeDtypeStruct((batch, seq_len, d), F32),
        compiler_params=_params2(4 * tm * d * 4 + 4 * tm * d * 2 + 2 * d * d * 2 + 6 * tm * d * 4 + (8 << 20)),
        name="dft3",
    )(x.reshape(batch, seq_len, d), yr.reshape(batch, t1n, t2n * d), yi.reshape(batch, t1n, t2n * d),
      mod, cs, w_out, lng, lnb)
    return out.reshape(n, d)


FFN_TM = 1024
FFN_SUB = 256
PROJ_TM = 512
MIX_TM = 512


def kernel(x, c, ctx, c_ctx, ada_w, ada_b, ln_g, ln_b, ffn1_w1, ffn1_w3, ffn1_w2, ffn2_w1, ffn2_w3, ffn2_w2,
           mix_w_in, pool_w, pool_scale, attn_sinks, mix_w_out, fourier_w_out):
    batch, seq_len, d = x.shape
    n_ctx = ctx.shape[1]
    assert d == D_MODEL and seq_len % MIX_TM == 0 and seq_len % GRID_W == 0 and n_ctx % BLOCK == 0

    c_rows = jnp.zeros((2 * 8, d), F32).at[:batch].set(c).at[batch].set(c_ctx)
    mod_all = _ada(c_rows, ada_w, ada_b).reshape(DEPTH, 2 * 8, N_MOD, d)

    xf = x.reshape(batch * seq_len, d)
    cf = ctx.reshape(batch * n_ctx, d)
    bf = lambda w: w.astype(BF16)
    for l in range(DEPTH):
        mod = mod_all[l, :batch]
        lng = ln_g[l][:, None, :]
        lnb = ln_b[l][:, None, :]
        ffn1 = (bf(ffn1_w1[l]), bf(ffn1_w3[l]), bf(ffn1_w2[l]))
        xf = _ffn(xf, mod[:, 0:3], lng[0], lnb[0], *ffn1, seq_len=seq_len, tm=FFN_TM, sub=FFN_SUB)
        if l % 2 == 0:
            e = l // 2
            mod_c = mod_all[l, batch:batch + 1]
            cf = _ffn(cf, mod_c[:, 0:3], lng[0], lnb[0], *ffn1, seq_len=n_ctx, tm=n_ctx, sub=n_ctx)
            w_in = bf(mix_w_in[e])
            u = _proj(xf, mod[:, 3:6], w_in, _rope_tables(seq_len), seq_len=seq_len, tm=PROJ_TM)
            u_ctx = _proj(cf, mod_c[:, 3:6], w_in, _identity_rope_tables(n_ctx), seq_len=n_ctx, tm=n_ctx)
            xf = _mix0(xf, u, u_ctx, mod[:, 3:6], attn_sinks[e], bf(pool_w[e]), pool_scale[e][None, :],
                       bf(mix_w_out[e]), lng[1], lnb[1], seq_len=seq_len, n_ctx=n_ctx, tm=MIX_TM)
        else:
            xf = _fourier_mix(xf, mod[:, 3:6], bf(fourier_w_out[l // 2]), lng[1], lnb[1],
                              batch=batch, seq_len=seq_len)
        ffn2 = (bf(ffn2_w1[l]), bf(ffn2_w3[l]), bf(ffn2_w2[l]))
        xf = _ffn(xf, mod[:, 6:9], lng[2], lnb[2], *ffn2, seq_len=seq_len, tm=FFN_TM, sub=FFN_SUB)
    return xf.reshape(batch, seq_len, d)
```

```python
import functools

import numpy as np
import jax
import jax.numpy as jnp
from jax import lax
from jax.experimental import pallas as pl
from jax.experimental.pallas import tpu as pltpu

F32 = jnp.float32
BF16 = jnp.bfloat16

D_MODEL = 1024
DEPTH = 2
GRID_W = 64
POOL_GROUPS = 4
POOL_WINDOWS = (2, 4, 8, 16)
POOL_DIM = D_MODEL // 2
POOL_GDIM = POOL_DIM // POOL_GROUPS
HEAD_DIM = 64
N_Q_HEADS = (D_MODEL // 2) // HEAD_DIM
N_KV_HEADS = 2
GQA_GROUP = N_Q_HEADS // N_KV_HEADS
Q_DIM = N_Q_HEADS * HEAD_DIM
KV_DIM = N_KV_HEADS * HEAD_DIM
WINDOW = 128
BLOCK = 128
ROPE_BASE = 10000.0
ROPE_HALF = HEAD_DIM // 2
FOURIER_GROUPS = 4
FOURIER_GDIM = D_MODEL // FOURIER_GROUPS
N_MOD = 9
ALPHA = (2 * DEPTH) ** 0.25
LN_EPS = 1e-5
ATTN_SCALE = HEAD_DIM ** -0.5
LOG2E = 1.4426950408889634

LANES = 128
BF16_SUBLANES = 16
V7X_VMEM_BYTES = 64 * 1024 * 1024

U_Q = N_Q_HEADS * LANES
U_P = POOL_DIM
U_KV = 4 * LANES
U_COLS = U_Q + U_P + U_KV
POOL_HALO = BF16_SUBLANES
NEG_BIG = -1e30


def _ln(x):
    mu = jnp.mean(x, axis=-1, keepdims=True)
    xc = x - mu
    var = jnp.mean(xc * xc, axis=-1, keepdims=True)
    return xc * lax.rsqrt(var + LN_EPS)


def _modulate_bf16(x, mod_ref):
    return (_ln(x) * (1.0 + mod_ref[1:2, :]) + mod_ref[0:1, :]).astype(BF16)


def _params(vmem_bytes):
    return pltpu.CompilerParams(dimension_semantics=("parallel",), vmem_limit_bytes=int(vmem_bytes))


def _params2(vmem_bytes):
    return pltpu.CompilerParams(dimension_semantics=("parallel", "parallel"), vmem_limit_bytes=int(vmem_bytes))


def _resident(shape):
    nd = len(shape)
    return pl.BlockSpec(shape, lambda *_: (0,) * nd, pipeline_mode=pl.Buffered(1))


def _ada_kernel(c_ref, w_ref, b_ref, o_ref):
    c = c_ref[...]
    s = c * jax.nn.sigmoid(c)
    o_ref[...] = jnp.dot(s, w_ref[...], preferred_element_type=F32) + b_ref[...]


def _ada(c_rows, ada_w, ada_b):
    r, d = c_rows.shape
    depth, _, n = ada_w.shape
    tn = n // 8
    return pl.pallas_call(
        _ada_kernel,
        grid=(depth, n // tn),
        in_specs=[
            pl.BlockSpec((r, d), lambda l, j: (0, 0)),
            pl.BlockSpec((None, d, tn), lambda l, j: (l, 0, j)),
            pl.BlockSpec((None, 1, tn), lambda l, j: (l, 0, j)),
        ],
        out_specs=pl.BlockSpec((None, r, tn), lambda l, j: (l, 0, j)),
        out_shape=jax.ShapeDtypeStruct((depth, r, n), F32),
        compiler_params=_params2(2 * d * tn * 4 + (4 << 20)),
        name="ada",
    )(c_rows, ada_w, ada_b.reshape(depth, 1, n))


def _ffn_kernel(x_ref, mod_ref, lng_ref, lnb_ref, w1_ref, w3_ref, w2_ref, o_ref, *, sub):
    for r in range(x_ref.shape[0] // sub):
        rows = slice(r * sub, (r + 1) * sub)
        x = x_ref[rows, :]
        h = _modulate_bf16(x, mod_ref)
        a = jnp.dot(h, w1_ref[...], preferred_element_type=F32)
        b = jnp.dot(h, w3_ref[...], preferred_element_type=F32)
        act = (a * jax.nn.sigmoid(a) * b).astype(BF16)
        y = jnp.dot(act, w2_ref[...], preferred_element_type=F32)
        z = ALPHA * x + (0.5 * mod_ref[2:3, :]) * y
        o_ref[rows, :] = _ln(z) * lng_ref[...] + lnb_ref[...]


def _ffn(x, mod, lng, lnb, w1, w3, w2, *, seq_len, tm, sub):
    n, d = x.shape
    f = w1.shape[1]
    tiles_per_seq = seq_len // tm
    mod_idx = (lambda i: (0, 0, 0)) if mod.shape[0] == 1 else (lambda i: (i // tiles_per_seq, 0, 0))
    vmem = 4 * tm * d * 4 + 3 * d * f * 2 + 2 * sub * f * 10 + (4 << 20)
    return pl.pallas_call(
        functools.partial(_ffn_kernel, sub=sub),
        grid=(n // tm,),
        in_specs=[
            pl.BlockSpec((tm, d), lambda i: (i, 0)),
            pl.BlockSpec((None, 3, d), mod_idx),
            _resident((1, d)),
            _resident((1, d)),
            _resident((d, f)),
            _resident((d, f)),
            _resident((f, d)),
        ],
        out_specs=pl.BlockSpec((tm, d), lambda i: (i, 0)),
        out_shape=jax.ShapeDtypeStruct((n, d), F32),
        compiler_params=_params(vmem),
        name="ffn",
    )(x, mod, lng, lnb, w1, w3, w2)


def _proj_kernel(x_ref, mod_ref, w_ref, cos_ref, sa_ref, sb_ref, o_ref):
    tm = x_ref.shape[0]
    h = _modulate_bf16(x_ref[...], mod_ref)
    u = jnp.dot(h, w_ref[...], preferred_element_type=F32)
    cos, sa, sb = cos_ref[...], sa_ref[...], sb_ref[...]
    low = lax.broadcasted_iota(jnp.int32, (tm, LANES), 1) < HEAD_DIM

    def rope(t):
        return t * cos + pltpu.roll(t, LANES - ROPE_HALF // 2, 1) * sa + pltpu.roll(t, ROPE_HALF // 2, 1) * sb

    for c in range(Q_DIM // LANES):
        qc = rope(u[:, POOL_DIM + c * LANES:POOL_DIM + (c + 1) * LANES]) * (ATTN_SCALE * LOG2E)
        o_ref[:, (2 * c) * LANES:(2 * c + 1) * LANES] = jnp.where(low, qc, 0.0).astype(BF16)
        o_ref[:, (2 * c + 1) * LANES:(2 * c + 2) * LANES] = jnp.where(low, 0.0, qc).astype(BF16)
    o_ref[:, U_Q:U_Q + U_P] = u[:, :POOL_DIM].astype(BF16)
    k = rope(u[:, POOL_DIM + Q_DIM:POOL_DIM + Q_DIM + KV_DIM])
    v = u[:, POOL_DIM + Q_DIM + KV_DIM:]
    base = U_Q + U_P
    for j, t in enumerate((k, v)):
        ts = pltpu.roll(t, HEAD_DIM, 1)
        o_ref[:, base + (2 * j) * LANES:base + (2 * j + 1) * LANES] = jnp.where(low, t, ts).astype(BF16)
        o_ref[:, base + (2 * j + 1) * LANES:base + (2 * j + 2) * LANES] = jnp.where(low, ts, t).astype(BF16)


def _proj(x, mod, w_in, rope_tabs, *, seq_len, tm):
    n, d = x.shape
    tiles_per_seq = seq_len // tm
    mod_idx = (lambda i: (0, 0, 0)) if mod.shape[0] == 1 else (lambda i: (i // tiles_per_seq, 0, 0))
    tab_spec = pl.BlockSpec((tm, LANES), lambda i: (i % tiles_per_seq, 0))
    vmem = 2 * tm * d * 4 + 2 * tm * U_COLS * 2 + 2 * d * w_in.shape[1] * 2 + tm * w_in.shape[1] * 8 + (8 << 20)
    return pl.pallas_call(
        _proj_kernel,
        grid=(n // tm,),
        in_specs=[
            pl.BlockSpec((tm, d), lambda i: (i, 0)),
            pl.BlockSpec((None, 3, d), mod_idx),
            _resident(w_in.shape),
            tab_spec, tab_spec, tab_spec,
        ],
        out_specs=pl.BlockSpec((tm, U_COLS), lambda i: (i, 0)),
        out_shape=jax.ShapeDtypeStruct((n, U_COLS), BF16),
        compiler_params=_params(vmem),
        name="proj",
    )(x, mod, w_in, *rope_tabs)


def _rope_tables(seq_len):
    rows = seq_len // GRID_W
    row = jnp.repeat(jnp.arange(rows), GRID_W).astype(F32)
    col = jnp.tile(jnp.arange(GRID_W), rows).astype(F32)
    freqs = ROPE_BASE ** (-jnp.arange(0, ROPE_HALF, 2, dtype=F32) / ROPE_HALF)
    ang_row = row[:, None] * freqs[None]
    ang_col = col[:, None] * freqs[None]
    cr, sr, cc, sc = jnp.cos(ang_row), jnp.sin(ang_row), jnp.cos(ang_col), jnp.sin(ang_col)
    z = jnp.zeros_like(sr)
    cos = jnp.concatenate([cr, cr, cc, cc], axis=-1)
    sa = jnp.concatenate([-sr, z, -sc, z], axis=-1)
    sb = jnp.concatenate([z, sr, z, sc], axis=-1)
    return tuple(jnp.tile(t, (1, LANES // HEAD_DIM)) for t in (cos, sa, sb))


def _identity_rope_tables(seq_len):
    return (jnp.ones((seq_len, LANES), F32), jnp.zeros((seq_len, LANES), F32), jnp.zeros((seq_len, LANES), F32))


def _pool_band():
    r = np.arange(BLOCK)[:, None]
    j = np.arange(BLOCK + 2 * POOL_HALO)[None, :] - POOL_HALO
    return np.stack([((j >= r - w // 2) & (j < r + w // 2)) for w in POOL_WINDOWS]).astype(np.float32)


def _band_bias():
    qi = np.tile(np.arange(BLOCK), GQA_GROUP)[:, None]
    ki = np.arange(BLOCK)[None, :]
    prev = np.where(ki - BLOCK >= qi - WINDOW, 0.0, NEG_BIG)
    nxt = np.where(ki + BLOCK <= qi + WINDOW, 0.0, NEG_BIG)
    return np.stack([prev, nxt]).astype(np.float32)


def _mix0_kernel(sinks_ref, x_ref, q_ref, pprev_ref, pcur_ref, pnext_ref, kvprev_ref, kvcur_ref, kvnext_ref,
                 kvctx_ref, mod_ref, band_ref, bias_ref, wpool_ref, pscale_ref, wout_ref, lng_ref, lnb_ref,
                 o_ref, cat_ref, *, tiles_per_seq, seq_len):
    tm = x_ref.shape[0]
    nb = tm // BLOCK
    assert nb >= 2
    n_ctx = kvctx_ref.shape[0]
    n_keys = 3 * BLOCK + n_ctx
    it = pl.program_id(0) % tiles_per_seq
    first = it == 0
    last = it == tiles_per_seq - 1

    halo_zero = jnp.zeros((POOL_HALO, POOL_DIM), BF16)
    pext = jnp.concatenate([jnp.where(first, halo_zero, pprev_ref[...]), pcur_ref[...],
                            jnp.where(last, halo_zero, pnext_ref[...])], axis=0)
    row = lax.broadcasted_iota(jnp.int32, (BLOCK, POOL_GDIM), 0)

    sg = [(s, g) for s in range(nb) for g in range(POOL_GROUPS)]
    wins = [jnp.dot(band_ref[g], pext[s * BLOCK:(s + 1) * BLOCK + 2 * POOL_HALO, g * POOL_GDIM:(g + 1) * POOL_GDIM],
                    preferred_element_type=F32) for s, g in sg]
    dlts = []
    for (s, g), win in zip(sg, wins):
        w = POOL_WINDOWS[g]
        t = it * tm + s * BLOCK + row
        cnt = jnp.minimum(t + w // 2, seq_len) - jnp.maximum(t - w // 2, 0)
        pc = pcur_ref[s * BLOCK:(s + 1) * BLOCK, g * POOL_GDIM:(g + 1) * POOL_GDIM].astype(F32)
        dlts.append((win / cnt.astype(F32) - pc).astype(BF16))
    for (s, g), dlt in zip(sg, dlts):
        cols = slice(g * POOL_GDIM, (g + 1) * POOL_GDIM)
        a = jnp.dot(dlt, wpool_ref[g], preferred_element_type=F32) * pscale_ref[:, cols]
        cat_ref[s * BLOCK:(s + 1) * BLOCK, cols] = a.astype(BF16)

    bias_prev = bias_ref[0]
    bias_next = bias_ref[1]
    bias_prev0 = jnp.where(first, NEG_BIG, bias_prev)
    bias_next1 = jnp.where(last, NEG_BIG, bias_next)
    low = lax.broadcasted_iota(jnp.int32, (BLOCK, LANES), 1) < HEAD_DIM
    ones_v = jnp.ones((n_keys, LANES), BF16)

    def scores(h, s):
        kcol = slice(h * LANES, (h + 1) * LANES)
        rows = slice(s * BLOCK, (s + 1) * BLOCK)
        qst = jnp.concatenate([q_ref[rows, (h * GQA_GROUP + g) * LANES:(h * GQA_GROUP + g + 1) * LANES]
                               for g in range(GQA_GROUP)], axis=0)
        kext = jnp.concatenate([kvprev_ref[:, kcol], kvcur_ref[:, kcol], kvnext_ref[:, kcol]], axis=0)
        kcat = jnp.concatenate([kext[s * BLOCK:(s + 3) * BLOCK], kvctx_ref[:, kcol]], axis=0)
        return lax.dot_general(qst, kcat, (((1,), (1,)), ((), ())), preferred_element_type=F32)

    def finish(h, s, sc):
        vcol = slice((N_KV_HEADS + h) * LANES, (N_KV_HEADS + h + 1) * LANES)
        rows = slice(s * BLOCK, (s + 1) * BLOCK)
        vext = jnp.concatenate([kvprev_ref[:, vcol], kvcur_ref[:, vcol], kvnext_ref[:, vcol]], axis=0)
        vcat = jnp.concatenate([jnp.concatenate([vext[s * BLOCK:(s + 3) * BLOCK], kvctx_ref[:, vcol]], axis=0),
                                ones_v], axis=1)
        sink = jnp.concatenate([jnp.full((BLOCK, 1), sinks_ref[h * GQA_GROUP + g] * LOG2E, F32)
                                for g in range(GQA_GROUP)], axis=0)
        bp = bias_prev0 if s == 0 else bias_prev
        bn = bias_next1 if s == nb - 1 else bias_next
        sc = jnp.concatenate([sc[:, :BLOCK] + bp, sc[:, BLOCK:2 * BLOCK], sc[:, 2 * BLOCK:3 * BLOCK] + bn,
                              sc[:, 3 * BLOCK:]], axis=1)
        m = jnp.maximum(jnp.max(sc, axis=-1, keepdims=True), sink)
        e = jnp.exp2(sc - m).astype(BF16)
        o2 = jnp.dot(e, vcat, preferred_element_type=F32)
        o = o2[:, :LANES] / (o2[:, LANES:] + jnp.exp2(sink - m))
        for j in range(GQA_GROUP // 2):
            pair = jnp.where(low, o[(2 * j) * BLOCK:(2 * j + 1) * BLOCK], o[(2 * j + 1) * BLOCK:(2 * j + 2) * BLOCK])
            c0 = POOL_DIM + (h * GQA_GROUP // 2 + j) * LANES
            cat_ref[rows, c0:c0 + LANES] = pair.astype(BF16)

    todo = [(h, s) for h in range(N_KV_HEADS) for s in range(nb)]
    sc_next = scores(*todo[0])
    for i, (h, s) in enumerate(todo):
        sc = sc_next
        if i + 1 < len(todo):
            sc_next = scores(*todo[i + 1])
        finish(h, s, sc)

    mix = jnp.dot(cat_ref[...], wout_ref[...], preferred_element_type=F32)
    z = ALPHA * x_ref[...] + mod_ref[2:3, :] * mix
    o_ref[...] = _ln(z) * lng_ref[...] + lnb_ref[...]


def _mix0(x, u, u_ctx, mod, sinks, w_pool, pool_scale, w_out, lng, lnb, *, seq_len, n_ctx, tm):
    n, d = x.shape
    nb = tm // BLOCK
    tiles_per_seq = seq_len // tm
    hb = tm // POOL_HALO
    q_blk, p_blk, kv_blk = 0, U_Q // U_P, (U_Q + U_P) // U_KV
    band = jnp.asarray(_pool_band(), BF16)
    bias = jnp.asarray(_band_bias(), F32)
    kernel = functools.partial(_mix0_kernel, tiles_per_seq=tiles_per_seq, seq_len=seq_len)
    vmem = 4 * tm * d * 4 + 4 * tm * U_COLS * 2 + 4 * d * d * 2 + 6 * GQA_GROUP * BLOCK * (3 * BLOCK + n_ctx) * 4 + (8 << 20)
    return pl.pallas_call(
        kernel,
        grid=(n // tm,),
        in_specs=[
            pl.BlockSpec(memory_space=pltpu.SMEM),
            pl.BlockSpec((tm, d), lambda i: (i, 0)),
            pl.BlockSpec((tm, U_Q), lambda i: (i, q_blk)),
            pl.BlockSpec((POOL_HALO, U_P), lambda i: (jnp.maximum(i * hb - 1, 0), p_blk)),
            pl.BlockSpec((tm, U_P), lambda i: (i, p_blk)),
            pl.BlockSpec((POOL_HALO, U_P), lambda i: (jnp.minimum((i + 1) * hb, n // POOL_HALO - 1), p_blk)),
            pl.BlockSpec((BLOCK, U_KV), lambda i: (jnp.maximum(i * nb - 1, 0), kv_blk)),
            pl.BlockSpec((tm, U_KV), lambda i: (i, kv_blk)),
            pl.BlockSpec((BLOCK, U_KV), lambda i: (jnp.minimum((i + 1) * nb, n // BLOCK - 1), kv_blk)),
            pl.BlockSpec((n_ctx, U_KV), lambda i: (i // tiles_per_seq, kv_blk)),
            pl.BlockSpec((None, 3, d), lambda i: (i // tiles_per_seq, 0, 0)),
            _resident(band.shape),
            _resident(bias.shape),
            _resident(w_pool.shape),
            _resident((1, POOL_DIM)),
            _resident(w_out.shape),
            _resident((1, d)),
            _resident((1, d)),
        ],
        out_specs=pl.BlockSpec((tm, d), lambda i: (i, 0)),
        out_shape=jax.ShapeDtypeStruct((n, d), F32),
        scratch_shapes=[pltpu.VMEM((tm, POOL_DIM + Q_DIM), BF16)],
        compiler_params=_params(vmem),
        name="mix0",
    )(sinks, x, u, u, u, u, u, u, u, u_ctx, mod, band, bias, w_pool, pool_scale, w_out, lng, lnb)


def _dft_tables(seq_len):
    t1n, t2n, c = seq_len // GRID_W, GRID_W, FOURIER_GDIM
    half = t1n // 2
    a1 = 2 * np.pi * (np.outer(np.arange(t1n), np.arange(t1n)) % t1n) / t1n
    m1 = np.concatenate([np.cos(a1[:half + 1]), np.sin(a1[1:half])], axis=0) / np.sqrt(t1n)

    t2 = np.arange(t2n)
    base = 2 * np.pi * (np.outer(t2, t2) % t2n) / t2n

    def amat(k1):
        ang = base + 2 * np.pi * k1 * t2[None, :] / seq_len
        return np.cos(ang) / np.sqrt(t2n), -np.sin(ang) / np.sqrt(t2n)

    zero = np.zeros((t2n, t2n))
    m2 = []
    for u in range(half):
        if u == 0:
            ar, ai = amat(0)
            br, bi = amat(half)
            m2.append(np.block([[ar, zero], [zero, br], [ai, zero], [zero, bi]]))
        else:
            ar, ai = amat(u)
            br, bi = amat(t1n - u)
            m2.append(np.block([[ar, ai], [br, -bi], [ai, -ar], [bi, br]]))
    m2 = np.stack(m2)
    a3 = 2 * np.pi * (np.outer(np.arange(c), np.arange(c)) % c) / c
    cs = np.concatenate([np.cos(a3), np.sin(a3)], axis=0) / np.sqrt(c)
    return tuple(jnp.asarray(t, F32).astype(BF16) for t in (m1, m2, cs))


def _dft1_kernel(x_hbm, mod_ref, m1_ref, z_hbm, xbuf, zbuf, in_sem, out_sem, *, nt2, steps_per_batch):
    n = pl.program_id(0)
    nsteps = pl.num_programs(0)
    slot = n % 2

    def where(step, i):
        return step // steps_per_batch, (step % steps_per_batch) * nt2 + i

    def load(step, sl, i):
        b, t2 = where(step, i)
        return pltpu.make_async_copy(x_hbm.at[b, :, t2, :], xbuf.at[sl, i], in_sem.at[sl, i])

    def store(step, sl, i):
        b, t2 = where(step, i)
        return pltpu.make_async_copy(zbuf.at[sl, i], z_hbm.at[b, :, t2, :], out_sem.at[sl, i])

    @pl.when(n == 0)
    def _():
        for i in range(nt2):
            load(n, slot, i).start()

    @pl.when(n + 1 < nsteps)
    def _():
        for i in range(nt2):
            load(n + 1, 1 - slot, i).start()

    @pl.when(n >= 2)
    def _():
        for i in range(nt2):
            store(n - 2, slot, i).wait()

    for i in range(nt2):
        load(n, slot, i).wait()
    hs = [_modulate_bf16(xbuf[slot, i], mod_ref) for i in range(nt2)]
    for i in range(nt2):
        zbuf[slot, i] = jnp.dot(m1_ref[...], hs[i], preferred_element_type=F32)
    for i in range(nt2):
        store(n, slot, i).start()

    @pl.when(n == nsteps - 1)
    def _():
        for i in range(nt2):
            store(n, slot, i).wait()

    @pl.when((n == nsteps - 1) & (n >= 1))
    def _():
        for i in range(nt2):
            store(n - 1, 1 - slot, i).wait()


def _dft23_kernel(x_hbm, zp_ref, zq_ref, m2_ref, mod_ref, cs_ref, wout_ref, lng_ref, lnb_ref, o_hbm,
                  xbuf, obuf, in_sem, out_sem, *, units, steps_per_batch, t1n):
    n = pl.program_id(0)
    nsteps = pl.num_programs(0)
    slot = n % 2
    t2n = zp_ref.shape[1]
    half = t1n // 2

    def where(step, g, side):
        u = (step % steps_per_batch) * units + g
        return step // steps_per_batch, (u if side == 0 else jnp.where(u == 0, half, t1n - u))

    def load(step, sl, g, side):
        b, k1 = where(step, g, side)
        return pltpu.make_async_copy(x_hbm.at[b, :, k1, :], xbuf.at[sl, g, side], in_sem.at[sl, g, side])

    def store(step, sl, g, side):
        b, k1 = where(step, g, side)
        return pltpu.make_async_copy(obuf.at[sl, g, side], o_hbm.at[b, :, k1, :], out_sem.at[sl, g, side])

    pairs = [(g, side) for g in range(units) for side in range(2)]

    @pl.when(n == 0)
    def _():
        for g, side in pairs:
            load(n, slot, g, side).start()

    @pl.when(n + 1 < nsteps)
    def _():
        for g, side in pairs:
            load(n + 1, 1 - slot, g, side).start()

    @pl.when(n >= 2)
    def _():
        for g, side in pairs:
            store(n - 2, slot, g, side).wait()

    for g, side in pairs:
        load(n, slot, g, side).wait()

    halves = [range(0, units // 2), range(units // 2, units)] if units >= 2 else [range(units)]
    ys = []
    for gs in halves:
        y = [jnp.dot(m2_ref[g], jnp.concatenate([zp_ref[g], zq_ref[g]], axis=0).astype(BF16),
                     preferred_element_type=F32) for g in gs]
        ys.append((jnp.concatenate([t[:2 * t2n] for t in y], axis=0).astype(BF16),
                   jnp.concatenate([t[2 * t2n:] for t in y], axis=0).astype(BF16)))
    fs = []
    for yr, yi in ys:
        parts = []
        for c in range(FOURIER_GROUPS):
            cols = slice(c * FOURIER_GDIM, (c + 1) * FOURIER_GDIM)
            parts.append(jnp.dot(jnp.concatenate([yr[:, cols], yi[:, cols]], axis=1), cs_ref[...],
                                 preferred_element_type=F32))
        fs.append(jnp.concatenate(parts, axis=1).astype(BF16))
    mixes = [jnp.dot(f, wout_ref[...], preferred_element_type=F32) for f in fs]
    for gs, mix in zip(halves, mixes):
        for g in gs:
            for side in range(2):
                r0 = (2 * (g - gs[0]) + side) * t2n
                z = ALPHA * xbuf[slot, g, side] + mod_ref[2:3, :] * mix[r0:r0 + t2n]
                obuf[slot, g, side] = _ln(z) * lng_ref[...] + lnb_ref[...]

    for g, side in pairs:
        store(n, slot, g, side).start()

    @pl.when(n == nsteps - 1)
    def _():
        for g, side in pairs:
            store(n, slot, g, side).wait()

    @pl.when((n == nsteps - 1) & (n >= 1))
    def _():
        for g, side in pairs:
            store(n - 1, 1 - slot, g, side).wait()


def _fourier_mix(x, mod, w_out, lng, lnb, *, batch, seq_len, nt2=4, units=4):
    n, d = x.shape
    t1n, t2n = seq_len // GRID_W, GRID_W
    half = t1n // 2
    assert t2n % nt2 == 0 and half % units == 0
    m1, m2, cs = _dft_tables(seq_len)
    any_spec = pl.BlockSpec(memory_space=pl.ANY)
    arb = lambda vmem: pltpu.CompilerParams(dimension_semantics=("arbitrary",), vmem_limit_bytes=int(vmem))

    spb1 = t2n // nt2
    z = pl.pallas_call(
        functools.partial(_dft1_kernel, nt2=nt2, steps_per_batch=spb1),
        grid=(batch * spb1,),
        in_specs=[
            any_spec,
            pl.BlockSpec((None, 3, d), lambda i: (i // spb1, 0, 0)),
            _resident(m1.shape),
        ],
        out_specs=any_spec,
        out_shape=jax.ShapeDtypeStruct((batch, t1n, t2n, d), F32),
        scratch_shapes=[pltpu.VMEM((2, nt2, t1n, d), F32), pltpu.VMEM((2, nt2, t1n, d), F32),
                        pltpu.SemaphoreType.DMA((2, nt2)), pltpu.SemaphoreType.DMA((2, nt2))],
        compiler_params=arb(4 * nt2 * t1n * d * 4 + 6 * t1n * d * 4 + (8 << 20)),
        name="dft1",
    )(x.reshape(batch, t1n, t2n, d), mod, m1)

    spb2 = half // units
    tok = units * 2 * t2n
    zblk = lambda off: pl.BlockSpec((None, units, t2n, d), lambda i: (i // spb2, off + i % spb2, 0, 0))
    out = pl.pallas_call(
        functools.partial(_dft23_kernel, units=units, steps_per_batch=spb2, t1n=t1n),
        grid=(batch * spb2,),
        in_specs=[
            any_spec,
            zblk(0),
            zblk(spb2),
            pl.BlockSpec((units, 4 * t2n, 2 * t2n), lambda i: (i % spb2, 0, 0)),
            pl.BlockSpec((None, 3, d), lambda i: (i // spb2, 0, 0)),
            _resident(cs.shape),
            _resident(w_out.shape),
            _resident((1, d)),
            _resident((1, d)),
        ],
        out_specs=any_spec,
        out_shape=jax.ShapeDtypeStruct((batch, t2n, t1n, d), F32),
        scratch_shapes=[pltpu.VMEM((2, units, 2, t2n, d), F32), pltpu.VMEM((2, units, 2, t2n, d), F32),
                        pltpu.SemaphoreType.DMA((2, units, 2)), pltpu.SemaphoreType.DMA((2, units, 2))],
        compiler_params=arb(4 * tok * d * 4 + 4 * tok * d * 4 + 2 * d * d * 2 + 10 * tok * d * 4 + (8 << 20)),
        name="dft23",
    )(x.reshape(batch, t2n, t1n, d), z, z, m2, mod, cs, w_out, lng, lnb)
    return out.reshape(n, d)


FFN_TM = 1024
FFN_SUB = 256
PROJ_TM = 512
MIX_TM = 512


def kernel(x, c, ctx, c_ctx, ada_w, ada_b, ln_g, ln_b, ffn1_w1, ffn1_w3, ffn1_w2, ffn2_w1, ffn2_w3, ffn2_w2,
           mix_w_in, pool_w, pool_scale, attn_sinks, mix_w_out, fourier_w_out):
    batch, seq_len, d = x.shape
    n_ctx = ctx.shape[1]
    assert d == D_MODEL and seq_len % FFN_TM == 0 and seq_len % MIX_TM == 0 and seq_len % GRID_W == 0
    assert n_ctx % BLOCK == 0

    c_rows = jnp.zeros((2 * 8, d), F32).at[:batch].set(c).at[batch].set(c_ctx)
    mod_all = _ada(c_rows, ada_w, ada_b).reshape(DEPTH, 2 * 8, N_MOD, d)

    xf = x.reshape(batch * seq_len, d)
    cf = ctx.reshape(batch * n_ctx, d)
    bf = lambda w: w.astype(BF16)
    for l in range(DEPTH):
        mod = mod_all[l, :batch]
        lng = ln_g[l][:, None, :]
        lnb = ln_b[l][:, None, :]
        ffn1 = (bf(ffn1_w1[l]), bf(ffn1_w3[l]), bf(ffn1_w2[l]))
        xf = _ffn(xf, mod[:, 0:3], lng[0], lnb[0], *ffn1, seq_len=seq_len, tm=FFN_TM, sub=FFN_SUB)
        if l % 2 == 0:
            e = l // 2
            mod_c = mod_all[l, batch:batch + 1]
            cf = _ffn(cf, mod_c[:, 0:3], lng[0], lnb[0], *ffn1, seq_len=n_ctx, tm=n_ctx, sub=n_ctx)
            w_in = bf(mix_w_in[e])
            u = _proj(xf, mod[:, 3:6], w_in, _rope_tables(seq_len), seq_len=seq_len, tm=PROJ_TM)
            u_ctx = _proj(cf, mod_c[:, 3:6], w_in, _identity_rope_tables(n_ctx), seq_len=n_ctx, tm=n_ctx)
            xf = _mix0(xf, u, u_ctx, mod[:, 3:6], attn_sinks[e], bf(pool_w[e]), pool_scale[e][None, :],
                       bf(mix_w_out[e]), lng[1], lnb[1], seq_len=seq_len, n_ctx=n_ctx, tm=MIX_TM)
        else:
            xf = _fourier_mix(xf, mod[:, 3:6], bf(fourier_w_out[l // 2]), lng[1], lnb[1],
                              batch=batch, seq_len=seq_len)
        ffn2 = (bf(ffn2_w1[l]), bf(ffn2_w3[l]), bf(ffn2_w2[l]))
        xf = _ffn(xf, mod[:, 6:9], lng[2], lnb[2], *ffn2, seq_len=seq_len, tm=FFN_TM, sub=FFN_SUB)
    return xf.reshape(batch, seq_len, d)
```

```python
import functools

import numpy as np
import jax
import jax.numpy as jnp
from jax import lax
from jax.experimental import pallas as pl
from jax.experimental.pallas import tpu as pltpu

F32 = jnp.float32
BF16 = jnp.bfloat16

D_MODEL = 1024
DEPTH = 2
GRID_W = 64
POOL_GROUPS = 4
POOL_WINDOWS = (2, 4, 8, 16)
POOL_DIM = D_MODEL // 2
POOL_GDIM = POOL_DIM // POOL_GROUPS
HEAD_DIM = 64
N_Q_HEADS = (D_MODEL // 2) // HEAD_DIM
N_KV_HEADS = 2
GQA_GROUP = N_Q_HEADS // N_KV_HEADS
Q_DIM = N_Q_HEADS * HEAD_DIM
KV_DIM = N_KV_HEADS * HEAD_DIM
WINDOW = 128
BLOCK = 128
ROPE_BASE = 10000.0
ROPE_HALF = HEAD_DIM // 2
FOURIER_GROUPS = 4
FOURIER_GDIM = D_MODEL // FOURIER_GROUPS
N_MOD = 9
ALPHA = (2 * DEPTH) ** 0.25
LN_EPS = 1e-5
ATTN_SCALE = HEAD_DIM ** -0.5
LOG2E = 1.4426950408889634

LANES = 128
BF16_SUBLANES = 16
V7X_VMEM_BYTES = 64 * 1024 * 1024

U_Q = N_Q_HEADS * LANES
U_P = POOL_DIM
U_KV = 4 * LANES
U_COLS = U_Q + U_P + U_KV
POOL_HALO = BF16_SUBLANES
NEG_BIG = -1e30


def _ln(x):
    mu = jnp.mean(x, axis=-1, keepdims=True)
    xc = x - mu
    var = jnp.mean(xc * xc, axis=-1, keepdims=True)
    return xc * lax.rsqrt(var + LN_EPS)


def _modulate_bf16(x, mod_ref):
    return (_ln(x) * (1.0 + mod_ref[1:2, :]) + mod_ref[0:1, :]).astype(BF16)


def _params(vmem_bytes):
    return pltpu.CompilerParams(dimension_semantics=("parallel",), vmem_limit_bytes=int(vmem_bytes))


def _params2(vmem_bytes):
    return pltpu.CompilerParams(dimension_semantics=("parallel", "parallel"), vmem_limit_bytes=int(vmem_bytes))


def _resident(shape):
    nd = len(shape)
    return pl.BlockSpec(shape, lambda *_: (0,) * nd, pipeline_mode=pl.Buffered(1))


def _ada_kernel(c_ref, w_ref, b_ref, o_ref):
    c = c_ref[...]
    s = c * jax.nn.sigmoid(c)
    o_ref[...] = jnp.dot(s, w_ref[...], preferred_element_type=F32) + b_ref[...]


def _ada(c_rows, ada_w, ada_b):
    r, d = c_rows.shape
    depth, _, n = ada_w.shape
    tn = n // 8
    return pl.pallas_call(
        _ada_kernel,
        grid=(depth, n // tn),
        in_specs=[
            pl.BlockSpec((r, d), lambda l, j: (0, 0)),
            pl.BlockSpec((None, d, tn), lambda l, j: (l, 0, j)),
            pl.BlockSpec((None, 1, tn), lambda l, j: (l, 0, j)),
        ],
        out_specs=pl.BlockSpec((None, r, tn), lambda l, j: (l, 0, j)),
        out_shape=jax.ShapeDtypeStruct((depth, r, n), F32),
        compiler_params=_params2(2 * d * tn * 4 + (4 << 20)),
        name="ada",
    )(c_rows, ada_w, ada_b.reshape(depth, 1, n))


def _ffn_kernel(x_ref, mod_ref, lng_ref, lnb_ref, w1_ref, w3_ref, w2_ref, o_ref, *, sub):
    for r in range(x_ref.shape[0] // sub):
        rows = slice(r * sub, (r + 1) * sub)
        x = x_ref[rows, :]
        h = _modulate_bf16(x, mod_ref)
        a = jnp.dot(h, w1_ref[...], preferred_element_type=F32)
        b = jnp.dot(h, w3_ref[...], preferred_element_type=F32)
        act = (a * jax.nn.sigmoid(a) * b).astype(BF16)
        y = jnp.dot(act, w2_ref[...], preferred_element_type=F32)
        z = ALPHA * x + (0.5 * mod_ref[2:3, :]) * y
        o_ref[rows, :] = _ln(z) * lng_ref[...] + lnb_ref[...]


def _ffn(x, mod, lng, lnb, w1, w3, w2, *, seq_len, tm, sub):
    n, d = x.shape
    f = w1.shape[1]
    tiles_per_seq = seq_len // tm
    mod_idx = (lambda i: (0, 0, 0)) if mod.shape[0] == 1 else (lambda i: (i // tiles_per_seq, 0, 0))
    vmem = 4 * tm * d * 4 + 3 * d * f * 2 + 2 * sub * f * 10 + (4 << 20)
    return pl.pallas_call(
        functools.partial(_ffn_kernel, sub=sub),
        grid=(n // tm,),
        in_specs=[
            pl.BlockSpec((tm, d), lambda i: (i, 0)),
            pl.BlockSpec((None, 3, d), mod_idx),
            _resident((1, d)),
            _resident((1, d)),
            _resident((d, f)),
            _resident((d, f)),
            _resident((f, d)),
        ],
        out_specs=pl.BlockSpec((tm, d), lambda i: (i, 0)),
        out_shape=jax.ShapeDtypeStruct((n, d), F32),
        compiler_params=_params(vmem),
        name="ffn",
    )(x, mod, lng, lnb, w1, w3, w2)


def _proj_kernel(x_ref, mod_ref, w_ref, cos_ref, sa_ref, sb_ref, o_ref, *, sub):
    low = lax.broadcasted_iota(jnp.int32, (sub, LANES), 1) < HEAD_DIM
    for r in range(x_ref.shape[0] // sub):
        rows = slice(r * sub, (r + 1) * sub)
        h = _modulate_bf16(x_ref[rows, :], mod_ref)
        u = jnp.dot(h, w_ref[...], preferred_element_type=F32)
        cos, sa, sb = cos_ref[rows, :], sa_ref[rows, :], sb_ref[rows, :]

        def rope(t):
            return t * cos + pltpu.roll(t, LANES - ROPE_HALF // 2, 1) * sa + pltpu.roll(t, ROPE_HALF // 2, 1) * sb

        for c in range(Q_DIM // LANES):
            qc = rope(u[:, POOL_DIM + c * LANES:POOL_DIM + (c + 1) * LANES]) * (ATTN_SCALE * LOG2E)
            o_ref[rows, (2 * c) * LANES:(2 * c + 1) * LANES] = jnp.where(low, qc, 0.0).astype(BF16)
            o_ref[rows, (2 * c + 1) * LANES:(2 * c + 2) * LANES] = jnp.where(low, 0.0, qc).astype(BF16)
        o_ref[rows, U_Q:U_Q + U_P] = u[:, :POOL_DIM].astype(BF16)
        k = rope(u[:, POOL_DIM + Q_DIM:POOL_DIM + Q_DIM + KV_DIM])
        v = u[:, POOL_DIM + Q_DIM + KV_DIM:]
        base = U_Q + U_P
        for j, t in enumerate((k, v)):
            ts = pltpu.roll(t, HEAD_DIM, 1)
            o_ref[rows, base + (2 * j) * LANES:base + (2 * j + 1) * LANES] = jnp.where(low, t, ts).astype(BF16)
            o_ref[rows, base + (2 * j + 1) * LANES:base + (2 * j + 2) * LANES] = jnp.where(low, ts, t).astype(BF16)


def _proj(x, mod, w_in, rope_tabs, *, seq_len, tm, sub):
    n, d = x.shape
    tiles_per_seq = seq_len // tm
    mod_idx = (lambda i: (0, 0, 0)) if mod.shape[0] == 1 else (lambda i: (i // tiles_per_seq, 0, 0))
    tab_spec = pl.BlockSpec((tm, LANES), lambda i: (i % tiles_per_seq, 0))
    vmem = 2 * tm * d * 4 + 2 * tm * U_COLS * 2 + 2 * d * w_in.shape[1] * 2 + tm * w_in.shape[1] * 8 + (8 << 20)
    return pl.pallas_call(
        functools.partial(_proj_kernel, sub=sub),
        grid=(n // tm,),
        in_specs=[
            pl.BlockSpec((tm, d), lambda i: (i, 0)),
            pl.BlockSpec((None, 3, d), mod_idx),
            _resident(w_in.shape),
            tab_spec, tab_spec, tab_spec,
        ],
        out_specs=pl.BlockSpec((tm, U_COLS), lambda i: (i, 0)),
        out_shape=jax.ShapeDtypeStruct((n, U_COLS), BF16),
        compiler_params=_params(vmem),
        name="proj",
    )(x, mod, w_in, *rope_tabs)


def _rope_tables(seq_len):
    rows = seq_len // GRID_W
    row = jnp.repeat(jnp.arange(rows), GRID_W).astype(F32)
    col = jnp.tile(jnp.arange(GRID_W), rows).astype(F32)
    freqs = ROPE_BASE ** (-jnp.arange(0, ROPE_HALF, 2, dtype=F32) / ROPE_HALF)
    ang_row = row[:, None] * freqs[None]
    ang_col = col[:, None] * freqs[None]
    cr, sr, cc, sc = jnp.cos(ang_row), jnp.sin(ang_row), jnp.cos(ang_col), jnp.sin(ang_col)
    z = jnp.zeros_like(sr)
    cos = jnp.concatenate([cr, cr, cc, cc], axis=-1)
    sa = jnp.concatenate([-sr, z, -sc, z], axis=-1)
    sb = jnp.concatenate([z, sr, z, sc], axis=-1)
    return tuple(jnp.tile(t, (1, LANES // HEAD_DIM)) for t in (cos, sa, sb))


def _identity_rope_tables(seq_len):
    return (jnp.ones((seq_len, LANES), F32), jnp.zeros((seq_len, LANES), F32), jnp.zeros((seq_len, LANES), F32))


def _pool_band():
    r = np.arange(BLOCK)[:, None]
    j = np.arange(BLOCK + 2 * POOL_HALO)[None, :] - POOL_HALO
    return np.stack([((j >= r - w // 2) & (j < r + w // 2)) for w in POOL_WINDOWS]).astype(np.float32)


def _band_bias():
    qi = np.tile(np.arange(BLOCK), GQA_GROUP)[:, None]
    ki = np.arange(BLOCK)[None, :]
    prev = np.where(ki - BLOCK >= qi - WINDOW, 0.0, NEG_BIG)
    nxt = np.where(ki + BLOCK <= qi + WINDOW, 0.0, NEG_BIG)
    return np.stack([prev, nxt]).astype(np.float32)


def _mix0_kernel(sinks_ref, x_ref, q_ref, pprev_ref, pcur_ref, pnext_ref, kvprev_ref, kvcur_ref, kvnext_ref,
                 kvctx_ref, mod_ref, band_ref, bias_ref, wpool_ref, pscale_ref, wout_ref, lng_ref, lnb_ref,
                 o_ref, cat_ref, *, tiles_per_seq, seq_len):
    tm = x_ref.shape[0]
    nb = tm // BLOCK
    assert nb >= 2
    n_ctx = kvctx_ref.shape[0]
    n_keys = 3 * BLOCK + n_ctx
    it = pl.program_id(0) % tiles_per_seq
    first = it == 0
    last = it == tiles_per_seq - 1

    halo_zero = jnp.zeros((POOL_HALO, POOL_DIM), BF16)
    pext = jnp.concatenate([jnp.where(first, halo_zero, pprev_ref[...]), pcur_ref[...],
                            jnp.where(last, halo_zero, pnext_ref[...])], axis=0)
    row = lax.broadcasted_iota(jnp.int32, (BLOCK, POOL_GDIM), 0)

    sg = [(s, g) for s in range(nb) for g in range(POOL_GROUPS)]
    wins = [jnp.dot(band_ref[g], pext[s * BLOCK:(s + 1) * BLOCK + 2 * POOL_HALO, g * POOL_GDIM:(g + 1) * POOL_GDIM],
                    preferred_element_type=F32) for s, g in sg]
    dlts = []
    for (s, g), win in zip(sg, wins):
        w = POOL_WINDOWS[g]
        t = it * tm + s * BLOCK + row
        cnt = jnp.minimum(t + w // 2, seq_len) - jnp.maximum(t - w // 2, 0)
        pc = pcur_ref[s * BLOCK:(s + 1) * BLOCK, g * POOL_GDIM:(g + 1) * POOL_GDIM].astype(F32)
        dlts.append((win / cnt.astype(F32) - pc).astype(BF16))
    for (s, g), dlt in zip(sg, dlts):
        cols = slice(g * POOL_GDIM, (g + 1) * POOL_GDIM)
        a = jnp.dot(dlt, wpool_ref[g], preferred_element_type=F32) * pscale_ref[:, cols]
        cat_ref[s * BLOCK:(s + 1) * BLOCK, cols] = a.astype(BF16)

    bias_prev = bias_ref[0]
    bias_next = bias_ref[1]
    bias_prev0 = jnp.where(first, NEG_BIG, bias_prev)
    bias_next1 = jnp.where(last, NEG_BIG, bias_next)
    low = lax.broadcasted_iota(jnp.int32, (BLOCK, LANES), 1) < HEAD_DIM
    ones_v = jnp.ones((n_keys, LANES), BF16)

    def scores(h, s):
        kcol = slice(h * LANES, (h + 1) * LANES)
        rows = slice(s * BLOCK, (s + 1) * BLOCK)
        qst = jnp.concatenate([q_ref[rows, (h * GQA_GROUP + g) * LANES:(h * GQA_GROUP + g + 1) * LANES]
                               for g in range(GQA_GROUP)], axis=0)
        kext = jnp.concatenate([kvprev_ref[:, kcol], kvcur_ref[:, kcol], kvnext_ref[:, kcol]], axis=0)
        kcat = jnp.concatenate([kext[s * BLOCK:(s + 3) * BLOCK], kvctx_ref[:, kcol]], axis=0)
        return lax.dot_general(qst, kcat, (((1,), (1,)), ((), ())), preferred_element_type=F32)

    def finish(h, s, sc):
        vcol = slice((N_KV_HEADS + h) * LANES, (N_KV_HEADS + h + 1) * LANES)
        rows = slice(s * BLOCK, (s + 1) * BLOCK)
        vext = jnp.concatenate([kvprev_ref[:, vcol], kvcur_ref[:, vcol], kvnext_ref[:, vcol]], axis=0)
        vcat = jnp.concatenate([jnp.concatenate([vext[s * BLOCK:(s + 3) * BLOCK], kvctx_ref[:, vcol]], axis=0),
                                ones_v], axis=1)
        sink = jnp.concatenate([jnp.full((BLOCK, 1), sinks_ref[h * GQA_GROUP + g] * LOG2E, F32)
                                for g in range(GQA_GROUP)], axis=0)
        bp = bias_prev0 if s == 0 else bias_prev
        bn = bias_next1 if s == nb - 1 else bias_next
        sc = jnp.concatenate([sc[:, :BLOCK] + bp, sc[:, BLOCK:2 * BLOCK], sc[:, 2 * BLOCK:3 * BLOCK] + bn,
                              sc[:, 3 * BLOCK:]], axis=1)
        m = jnp.maximum(jnp.max(sc, axis=-1, keepdims=True), sink)
        e = jnp.exp2(sc - m).astype(BF16)
        o2 = jnp.dot(e, vcat, preferred_element_type=F32)
        o = o2[:, :LANES] / (o2[:, LANES:] + jnp.exp2(sink - m))
        for j in range(GQA_GROUP // 2):
            pair = jnp.where(low, o[(2 * j) * BLOCK:(2 * j + 1) * BLOCK], o[(2 * j + 1) * BLOCK:(2 * j + 2) * BLOCK])
            c0 = POOL_DIM + (h * GQA_GROUP // 2 + j) * LANES
            cat_ref[rows, c0:c0 + LANES] = pair.astype(BF16)

    todo = [(h, s) for h in range(N_KV_HEADS) for s in range(nb)]
    sc_next = scores(*todo[0])
    for i, (h, s) in enumerate(todo):
        sc = sc_next
        if i + 1 < len(todo):
            sc_next = scores(*todo[i + 1])
        finish(h, s, sc)

    mix = jnp.dot(cat_ref[...], wout_ref[...], preferred_element_type=F32)
    z = ALPHA * x_ref[...] + mod_ref[2:3, :] * mix
    o_ref[...] = _ln(z) * lng_ref[...] + lnb_ref[...]


def _mix0(x, u, u_ctx, mod, sinks, w_pool, pool_scale, w_out, lng, lnb, *, seq_len, n_ctx, tm):
    n, d = x.shape
    nb = tm // BLOCK
    tiles_per_seq = seq_len // tm
    hb = tm // POOL_HALO
    q_blk, p_blk, kv_blk = 0, U_Q // U_P, (U_Q + U_P) // U_KV
    band = jnp.asarray(_pool_band(), BF16)
    bias = jnp.asarray(_band_bias(), F32)
    kernel = functools.partial(_mix0_kernel, tiles_per_seq=tiles_per_seq, seq_len=seq_len)
    vmem = 4 * tm * d * 4 + 4 * tm * U_COLS * 2 + 4 * d * d * 2 + 6 * GQA_GROUP * BLOCK * (3 * BLOCK + n_ctx) * 4 + (8 << 20)
    return pl.pallas_call(
        kernel,
        grid=(n // tm,),
        in_specs=[
            pl.BlockSpec(memory_space=pltpu.SMEM),
            pl.BlockSpec((tm, d), lambda i: (i, 0)),
            pl.BlockSpec((tm, U_Q), lambda i: (i, q_blk)),
            pl.BlockSpec((POOL_HALO, U_P), lambda i: (jnp.maximum(i * hb - 1, 0), p_blk)),
            pl.BlockSpec((tm, U_P), lambda i: (i, p_blk)),
            pl.BlockSpec((POOL_HALO, U_P), lambda i: (jnp.minimum((i + 1) * hb, n // POOL_HALO - 1), p_blk)),
            pl.BlockSpec((BLOCK, U_KV), lambda i: (jnp.maximum(i * nb - 1, 0), kv_blk)),
            pl.BlockSpec((tm, U_KV), lambda i: (i, kv_blk)),
            pl.BlockSpec((BLOCK, U_KV), lambda i: (jnp.minimum((i + 1) * nb, n // BLOCK - 1), kv_blk)),
            pl.BlockSpec((n_ctx, U_KV), lambda i: (i // tiles_per_seq, kv_blk)),
            pl.BlockSpec((None, 3, d), lambda i: (i // tiles_per_seq, 0, 0)),
            _resident(band.shape),
            _resident(bias.shape),
            _resident(w_pool.shape),
            _resident((1, POOL_DIM)),
            _resident(w_out.shape),
            _resident((1, d)),
            _resident((1, d)),
        ],
        out_specs=pl.BlockSpec((tm, d), lambda i: (i, 0)),
        out_shape=jax.ShapeDtypeStruct((n, d), F32),
        scratch_shapes=[pltpu.VMEM((tm, POOL_DIM + Q_DIM), BF16)],
        compiler_params=_params(vmem),
        name="mix0",
    )(sinks, x, u, u, u, u, u, u, u, u_ctx, mod, band, bias, w_pool, pool_scale, w_out, lng, lnb)


def _dft_tables(seq_len):
    t1n, t2n, c = seq_len // GRID_W, GRID_W, FOURIER_GDIM
    half = t1n // 2
    a1 = 2 * np.pi * (np.outer(np.arange(t1n), np.arange(t1n)) % t1n) / t1n
    m1 = np.concatenate([np.cos(a1[:half + 1]), np.sin(a1[1:half])], axis=0) / np.sqrt(t1n)

    t2 = np.arange(t2n)
    base = 2 * np.pi * (np.outer(t2, t2) % t2n) / t2n

    def amat(k1):
        ang = base + 2 * np.pi * k1 * t2[None, :] / seq_len
        return np.cos(ang) / np.sqrt(t2n), -np.sin(ang) / np.sqrt(t2n)

    zero = np.zeros((t2n, t2n))
    m2 = []
    for u in range(half):
        if u == 0:
            ar, ai = amat(0)
            br, bi = amat(half)
            m2.append(np.block([[ar, zero], [zero, br], [ai, zero], [zero, bi]]))
        else:
            ar, ai = amat(u)
            br, bi = amat(t1n - u)
            m2.append(np.block([[ar, ai], [br, -bi], [ai, -ar], [bi, br]]))
    m2 = np.stack(m2)
    a3 = 2 * np.pi * (np.outer(np.arange(c), np.arange(c)) % c) / c
    cs = np.concatenate([np.cos(a3), np.sin(a3)], axis=0) / np.sqrt(c)
    return tuple(jnp.asarray(t, F32).astype(BF16) for t in (m1, m2, cs))


def _dft1_kernel(x_hbm, mod_ref, m1_ref, z_hbm, xbuf, zbuf, in_sem, out_sem, *, nt2, steps_per_batch):
    n = pl.program_id(0)
    nsteps = pl.num_programs(0)
    slot = n % 2

    def where(step, i):
        return step // steps_per_batch, (step % steps_per_batch) * nt2 + i

    def load(step, sl, i):
        b, t2 = where(step, i)
        return pltpu.make_async_copy(x_hbm.at[b, :, t2, :], xbuf.at[sl, i], in_sem.at[sl])

    def store(step, sl, i):
        b, t2 = where(step, i)
        return pltpu.make_async_copy(zbuf.at[sl, i], z_hbm.at[b, :, t2, :], out_sem.at[sl])

    @pl.when(n == 0)
    def _():
        for i in range(nt2):
            load(n, slot, i).start()

    @pl.when(n + 1 < nsteps)
    def _():
        for i in range(nt2):
            load(n + 1, 1 - slot, i).start()

    @pl.when(n >= 2)
    def _():
        for i in range(nt2):
            store(n - 2, slot, i).wait()

    for i in range(nt2):
        load(n, slot, i).wait()
    hs = [_modulate_bf16(xbuf[slot, i], mod_ref) for i in range(nt2)]
    for i in range(nt2):
        zbuf[slot, i] = jnp.dot(m1_ref[...], hs[i], preferred_element_type=F32)
    for i in range(nt2):
        store(n, slot, i).start()

    @pl.when(n == nsteps - 1)
    def _():
        for i in range(nt2):
            store(n, slot, i).wait()

    @pl.when((n == nsteps - 1) & (n >= 1))
    def _():
        for i in range(nt2):
            store(n - 1, 1 - slot, i).wait()


def _dft23_kernel(x_hbm, zp_ref, zq_ref, m2_ref, mod_ref, cs_ref, wout_ref, lng_ref, lnb_ref, o_hbm,
                  xbuf, obuf, in_sem, out_sem, *, units, steps_per_batch, t1n):
    n = pl.program_id(0)
    nsteps = pl.num_programs(0)
    slot = n % 2
    t2n = zp_ref.shape[1]
    half = t1n // 2

    def where(step, g, side):
        u = (step % steps_per_batch) * units + g
        return step // steps_per_batch, (u if side == 0 else jnp.where(u == 0, half, t1n - u))

    def load(step, sl, g, side):
        b, k1 = where(step, g, side)
        return pltpu.make_async_copy(x_hbm.at[b, :, k1, :], xbuf.at[sl, g, side], in_sem.at[sl])

    def store(step, sl, g, side):
        b, k1 = where(step, g, side)
        return pltpu.make_async_copy(obuf.at[sl, g, side], o_hbm.at[b, :, k1, :], out_sem.at[sl])

    pairs = [(g, side) for g in range(units) for side in range(2)]

    @pl.when(n == 0)
    def _():
        for g, side in pairs:
            load(n, slot, g, side).start()

    @pl.when(n + 1 < nsteps)
    def _():
        for g, side in pairs:
            load(n + 1, 1 - slot, g, side).start()

    @pl.when(n >= 2)
    def _():
        for g, side in pairs:
            store(n - 2, slot, g, side).wait()

    for g, side in pairs:
        load(n, slot, g, side).wait()

    ngroups = min(units, 4)
    per = units // ngroups
    groups = [range(i * per, (i + 1) * per) for i in range(ngroups)]

    def stage2(gs):
        y = [jnp.dot(m2_ref[g], jnp.concatenate([zp_ref[g], zq_ref[g]], axis=0).astype(BF16),
                     preferred_element_type=F32) for g in gs]
        return (jnp.concatenate([t[:2 * t2n] for t in y], axis=0).astype(BF16),
                jnp.concatenate([t[2 * t2n:] for t in y], axis=0).astype(BF16))

    def project(yr, yi):
        parts = []
        for c in range(FOURIER_GROUPS):
            cols = slice(c * FOURIER_GDIM, (c + 1) * FOURIER_GDIM)
            parts.append(jnp.dot(jnp.concatenate([yr[:, cols], yi[:, cols]], axis=1), cs_ref[...],
                                 preferred_element_type=F32))
        f = jnp.concatenate(parts, axis=1).astype(BF16)
        return jnp.dot(f, wout_ref[...], preferred_element_type=F32)

    def epilogue(gs, mix):
        for g in gs:
            for side in range(2):
                r0 = (2 * (g - gs[0]) + side) * t2n
                z = ALPHA * xbuf[slot, g, side] + mod_ref[2:3, :] * mix[r0:r0 + t2n]
                obuf[slot, g, side] = _ln(z) * lng_ref[...] + lnb_ref[...]

    y_next = stage2(groups[0])
    pending = None
    for i, gs in enumerate(groups):
        y = y_next
        if i + 1 < ngroups:
            y_next = stage2(groups[i + 1])
        if pending is not None:
            epilogue(*pending)
        pending = (gs, project(*y))
    epilogue(*pending)

    for g, side in pairs:
        store(n, slot, g, side).start()

    @pl.when(n == nsteps - 1)
    def _():
        for g, side in pairs:
            store(n, slot, g, side).wait()

    @pl.when((n == nsteps - 1) & (n >= 1))
    def _():
        for g, side in pairs:
            store(n - 1, 1 - slot, g, side).wait()


def _fourier_mix(x, mod, w_out, lng, lnb, *, batch, seq_len, nt2=4, units=8):
    n, d = x.shape
    t1n, t2n = seq_len // GRID_W, GRID_W
    half = t1n // 2
    assert t2n % nt2 == 0 and half % units == 0 and units % min(units, 4) == 0
    m1, m2, cs = _dft_tables(seq_len)
    any_spec = pl.BlockSpec(memory_space=pl.ANY)
    arb = lambda vmem: pltpu.CompilerParams(dimension_semantics=("arbitrary",), vmem_limit_bytes=int(vmem))

    spb1 = t2n // nt2
    z = pl.pallas_call(
        functools.partial(_dft1_kernel, nt2=nt2, steps_per_batch=spb1),
        grid=(batch * spb1,),
        in_specs=[
            any_spec,
            pl.BlockSpec((None, 3, d), lambda i: (i // spb1, 0, 0)),
            _resident(m1.shape),
        ],
        out_specs=any_spec,
        out_shape=jax.ShapeDtypeStruct((batch, t1n, t2n, d), F32),
        scratch_shapes=[pltpu.VMEM((2, nt2, t1n, d), F32), pltpu.VMEM((2, nt2, t1n, d), F32),
                        pltpu.SemaphoreType.DMA((2,)), pltpu.SemaphoreType.DMA((2,))],
        compiler_params=arb(4 * nt2 * t1n * d * 4 + 6 * t1n * d * 4 + (8 << 20)),
        name="dft1",
    )(x.reshape(batch, t1n, t2n, d), mod, m1)

    spb2 = half // units
    tok = units * 2 * t2n
    zblk = lambda off: pl.BlockSpec((None, units, t2n, d), lambda i: (i // spb2, off + i % spb2, 0, 0))
    out = pl.pallas_call(
        functools.partial(_dft23_kernel, units=units, steps_per_batch=spb2, t1n=t1n),
        grid=(batch * spb2,),
        in_specs=[
            any_spec,
            zblk(0),
            zblk(spb2),
            pl.BlockSpec((units, 4 * t2n, 2 * t2n), lambda i: (i % spb2, 0, 0)),
            pl.BlockSpec((None, 3, d), lambda i: (i // spb2, 0, 0)),
            _resident(cs.shape),
            _resident(w_out.shape),
            _resident((1, d)),
            _resident((1, d)),
        ],
        out_specs=any_spec,
        out_shape=jax.ShapeDtypeStruct((batch, t2n, t1n, d), F32),
        scratch_shapes=[pltpu.VMEM((2, units, 2, t2n, d), F32), pltpu.VMEM((2, units, 2, t2n, d), F32),
                        pltpu.SemaphoreType.DMA((2,)), pltpu.SemaphoreType.DMA((2,))],
        compiler_params=arb(4 * tok * d * 4 + 4 * tok * d * 4 + 2 * d * d * 2 + 10 * tok * d * 4 + (8 << 20)),
        name="dft23",
    )(x.reshape(batch, t2n, t1n, d), z, z, m2, mod, cs, w_out, lng, lnb)
    return out.reshape(n, d)


FFN_TM = 1024
FFN_SUB = 256
PROJ_TM = 1024
PROJ_SUB = 256
MIX_TM = 1024


def kernel(x, c, ctx, c_ctx, ada_w, ada_b, ln_g, ln_b, ffn1_w1, ffn1_w3, ffn1_w2, ffn2_w1, ffn2_w3, ffn2_w2,
           mix_w_in, pool_w, pool_scale, attn_sinks, mix_w_out, fourier_w_out):
    batch, seq_len, d = x.shape
    n_ctx = ctx.shape[1]
    assert d == D_MODEL and seq_len % GRID_W == 0
    assert seq_len % FFN_TM == 0 and seq_len % PROJ_TM == 0 and seq_len % MIX_TM == 0
    assert n_ctx % BLOCK == 0

    c_rows = jnp.zeros((2 * 8, d), F32).at[:batch].set(c).at[batch].set(c_ctx)
    mod_all = _ada(c_rows, ada_w, ada_b).reshape(DEPTH, 2 * 8, N_MOD, d)

    xf = x.reshape(batch * seq_len, d)
    cf = ctx.reshape(batch * n_ctx, d)
    bf = lambda w: w.astype(BF16)
    for l in range(DEPTH):
        mod = mod_all[l, :batch]
        lng = ln_g[l][:, None, :]
        lnb = ln_b[l][:, None, :]
        ffn1 = (bf(ffn1_w1[l]), bf(ffn1_w3[l]), bf(ffn1_w2[l]))
        xf = _ffn(xf, mod[:, 0:3], lng[0], lnb[0], *ffn1, seq_len=seq_len, tm=FFN_TM, sub=FFN_SUB)
        if l % 2 == 0:
            e = l // 2
            mod_c = mod_all[l, batch:batch + 1]
            cf = _ffn(cf, mod_c[:, 0:3], lng[0], lnb[0], *ffn1, seq_len=n_ctx, tm=n_ctx, sub=n_ctx)
            w_in = bf(mix_w_in[e])
            u = _proj(xf, mod[:, 3:6], w_in, _rope_tables(seq_len), seq_len=seq_len, tm=PROJ_TM, sub=PROJ_SUB)
            u_ctx = _proj(cf, mod_c[:, 3:6], w_in, _identity_rope_tables(n_ctx), seq_len=n_ctx, tm=n_ctx, sub=n_ctx)
            xf = _mix0(xf, u, u_ctx, mod[:, 3:6], attn_sinks[e], bf(pool_w[e]), pool_scale[e][None, :],
                       bf(mix_w_out[e]), lng[1], lnb[1], seq_len=seq_len, n_ctx=n_ctx, tm=MIX_TM)
        else:
            xf = _fourier_mix(xf, mod[:, 3:6], bf(fourier_w_out[l // 2]), lng[1], lnb[1],
                              batch=batch, seq_len=seq_len)
        ffn2 = (bf(ffn2_w1[l]), bf(ffn2_w3[l]), bf(ffn2_w2[l]))
        xf = _ffn(xf, mod[:, 6:9], lng[2], lnb[2], *ffn2, seq_len=seq_len, tm=FFN_TM, sub=FFN_SUB)
    return xf.reshape(batch, seq_len, d)
```

```python
import functools

import numpy as np
import jax
import jax.numpy as jnp
from jax import lax
from jax.experimental import pallas as pl
from jax.experimental.pallas import tpu as pltpu

F32 = jnp.float32
BF16 = jnp.bfloat16

D_MODEL = 1024
DEPTH = 2
GRID_W = 64
POOL_GROUPS = 4
POOL_WINDOWS = (2, 4, 8, 16)
POOL_DIM = D_MODEL // 2
POOL_GDIM = POOL_DIM // POOL_GROUPS
HEAD_DIM = 64
N_Q_HEADS = (D_MODEL // 2) // HEAD_DIM
N_KV_HEADS = 2
GQA_GROUP = N_Q_HEADS // N_KV_HEADS
Q_DIM = N_Q_HEADS * HEAD_DIM
KV_DIM = N_KV_HEADS * HEAD_DIM
WINDOW = 128
BLOCK = 128
ROPE_BASE = 10000.0
ROPE_HALF = HEAD_DIM // 2
FOURIER_GROUPS = 4
FOURIER_GDIM = D_MODEL // FOURIER_GROUPS
N_MOD = 9
ALPHA = (2 * DEPTH) ** 0.25
LN_EPS = 1e-5
ATTN_SCALE = HEAD_DIM ** -0.5
LOG2E = 1.4426950408889634

LANES = 128
BF16_SUBLANES = 16
V7X_VMEM_BYTES = 64 * 1024 * 1024

U_Q = N_Q_HEADS * LANES
U_P = POOL_DIM
U_KV = 4 * LANES
U_COLS = U_Q + U_P + U_KV
POOL_HALO = BF16_SUBLANES
NEG_BIG = -1e30


def _ln(x):
    mu = jnp.mean(x, axis=-1, keepdims=True)
    xc = x - mu
    var = jnp.mean(xc * xc, axis=-1, keepdims=True)
    return xc * lax.rsqrt(var + LN_EPS)


def _modulate_bf16(x, mod_ref):
    return (_ln(x) * (1.0 + mod_ref[1:2, :]) + mod_ref[0:1, :]).astype(BF16)


def _params(vmem_bytes):
    return pltpu.CompilerParams(dimension_semantics=("parallel",), vmem_limit_bytes=int(vmem_bytes))


def _params2(vmem_bytes):
    return pltpu.CompilerParams(dimension_semantics=("parallel", "parallel"), vmem_limit_bytes=int(vmem_bytes))


def _resident(shape):
    nd = len(shape)
    return pl.BlockSpec(shape, lambda *_: (0,) * nd, pipeline_mode=pl.Buffered(1))


def _ada_kernel(c_ref, w_ref, b_ref, o_ref):
    c = c_ref[...]
    s = c * jax.nn.sigmoid(c)
    o_ref[...] = jnp.dot(s, w_ref[...], preferred_element_type=F32) + b_ref[...]


def _ada(c_rows, ada_w, ada_b):
    r, d = c_rows.shape
    depth, _, n = ada_w.shape
    tn = n // 8
    return pl.pallas_call(
        _ada_kernel,
        grid=(depth, n // tn),
        in_specs=[
            pl.BlockSpec((r, d), lambda l, j: (0, 0)),
            pl.BlockSpec((None, d, tn), lambda l, j: (l, 0, j)),
            pl.BlockSpec((None, 1, tn), lambda l, j: (l, 0, j)),
        ],
        out_specs=pl.BlockSpec((None, r, tn), lambda l, j: (l, 0, j)),
        out_shape=jax.ShapeDtypeStruct((depth, r, n), F32),
        compiler_params=_params2(2 * d * tn * 4 + (4 << 20)),
        name="ada",
    )(c_rows, ada_w, ada_b.reshape(depth, 1, n))


def _ffn_kernel(x_ref, mod_ref, lng_ref, lnb_ref, w1_ref, w3_ref, w2_ref, o_ref, *, sub):
    for r in range(x_ref.shape[0] // sub):
        rows = slice(r * sub, (r + 1) * sub)
        x = x_ref[rows, :]
        h = _modulate_bf16(x, mod_ref)
        a = jnp.dot(h, w1_ref[...], preferred_element_type=F32)
        b = jnp.dot(h, w3_ref[...], preferred_element_type=F32)
        act = (a * jax.nn.sigmoid(a) * b).astype(BF16)
        y = jnp.dot(act, w2_ref[...], preferred_element_type=F32)
        z = ALPHA * x + (0.5 * mod_ref[2:3, :]) * y
        o_ref[rows, :] = _ln(z) * lng_ref[...] + lnb_ref[...]


def _ffn(x, mod, lng, lnb, w1, w3, w2, *, layer, seq_len, tm, sub):
    n, d = x.shape
    f = w1.shape[2]
    layer_block = lambda rows, cols: pl.BlockSpec((None, rows, cols), lambda i: (layer, 0, 0),
                                                  pipeline_mode=pl.Buffered(1))
    tiles_per_seq = seq_len // tm
    mod_idx = (lambda i: (0, 0, 0)) if mod.shape[0] == 1 else (lambda i: (i // tiles_per_seq, 0, 0))
    vmem = min(4 * tm * d * 4 + 3 * d * f * 2 + (tm // sub) * (9 << 17) + (1 << 20), V7X_VMEM_BYTES)
    return pl.pallas_call(
        functools.partial(_ffn_kernel, sub=sub),
        grid=(n // tm,),
        in_specs=[
            pl.BlockSpec((tm, d), lambda i: (i, 0)),
            pl.BlockSpec((None, 3, d), mod_idx),
            _resident((1, d)),
            _resident((1, d)),
            layer_block(d, f),
            layer_block(d, f),
            layer_block(f, d),
        ],
        out_specs=pl.BlockSpec((tm, d), lambda i: (i, 0)),
        out_shape=jax.ShapeDtypeStruct((n, d), F32),
        compiler_params=_params(vmem),
        name="ffn",
    )(x, mod, lng, lnb, w1, w3, w2)


def _proj_kernel(x_ref, mod_ref, w_ref, cos_ref, sa_ref, sb_ref, o_ref, *, sub):
    low = lax.broadcasted_iota(jnp.int32, (sub, LANES), 1) < HEAD_DIM
    for r in range(x_ref.shape[0] // sub):
        rows = slice(r * sub, (r + 1) * sub)
        h = _modulate_bf16(x_ref[rows, :], mod_ref)
        u = jnp.dot(h, w_ref[...], preferred_element_type=F32)
        cos, sa, sb = cos_ref[rows, :], sa_ref[rows, :], sb_ref[rows, :]

        def rope(t):
            return t * cos + pltpu.roll(t, LANES - ROPE_HALF // 2, 1) * sa + pltpu.roll(t, ROPE_HALF // 2, 1) * sb

        for c in range(Q_DIM // LANES):
            qc = rope(u[:, POOL_DIM + c * LANES:POOL_DIM + (c + 1) * LANES]) * (ATTN_SCALE * LOG2E)
            o_ref[rows, (2 * c) * LANES:(2 * c + 1) * LANES] = jnp.where(low, qc, 0.0).astype(BF16)
            o_ref[rows, (2 * c + 1) * LANES:(2 * c + 2) * LANES] = jnp.where(low, 0.0, qc).astype(BF16)
        o_ref[rows, U_Q:U_Q + U_P] = u[:, :POOL_DIM].astype(BF16)
        k = rope(u[:, POOL_DIM + Q_DIM:POOL_DIM + Q_DIM + KV_DIM])
        v = u[:, POOL_DIM + Q_DIM + KV_DIM:]
        base = U_Q + U_P
        for j, t in enumerate((k, v)):
            ts = pltpu.roll(t, HEAD_DIM, 1)
            o_ref[rows, base + (2 * j) * LANES:base + (2 * j + 1) * LANES] = jnp.where(low, t, ts).astype(BF16)
            o_ref[rows, base + (2 * j + 1) * LANES:base + (2 * j + 2) * LANES] = jnp.where(low, ts, t).astype(BF16)


def _proj(x, mod, w_in, rope_tabs, *, seq_len, tm, sub):
    n, d = x.shape
    tiles_per_seq = seq_len // tm
    mod_idx = (lambda i: (0, 0, 0)) if mod.shape[0] == 1 else (lambda i: (i // tiles_per_seq, 0, 0))
    tab_spec = pl.BlockSpec((tm, LANES), lambda i: (i % tiles_per_seq, 0))
    vmem = 2 * tm * d * 4 + 2 * tm * U_COLS * 2 + 2 * d * w_in.shape[1] * 2 + tm * w_in.shape[1] * 8 + (8 << 20)
    return pl.pallas_call(
        functools.partial(_proj_kernel, sub=sub),
        grid=(n // tm,),
        in_specs=[
            pl.BlockSpec((tm, d), lambda i: (i, 0)),
            pl.BlockSpec((None, 3, d), mod_idx),
            _resident(w_in.shape),
            tab_spec, tab_spec, tab_spec,
        ],
        out_specs=pl.BlockSpec((tm, U_COLS), lambda i: (i, 0)),
        out_shape=jax.ShapeDtypeStruct((n, U_COLS), BF16),
        compiler_params=_params(vmem),
        name="proj",
    )(x, mod, w_in, *rope_tabs)


def _rope_tables(seq_len):
    rows = seq_len // GRID_W
    row = jnp.repeat(jnp.arange(rows), GRID_W).astype(F32)
    col = jnp.tile(jnp.arange(GRID_W), rows).astype(F32)
    freqs = ROPE_BASE ** (-jnp.arange(0, ROPE_HALF, 2, dtype=F32) / ROPE_HALF)
    ang_row = row[:, None] * freqs[None]
    ang_col = col[:, None] * freqs[None]
    cr, sr, cc, sc = jnp.cos(ang_row), jnp.sin(ang_row), jnp.cos(ang_col), jnp.sin(ang_col)
    z = jnp.zeros_like(sr)
    cos = jnp.concatenate([cr, cr, cc, cc], axis=-1)
    sa = jnp.concatenate([-sr, z, -sc, z], axis=-1)
    sb = jnp.concatenate([z, sr, z, sc], axis=-1)
    return tuple(jnp.tile(t, (1, LANES // HEAD_DIM)) for t in (cos, sa, sb))


def _identity_rope_tables(seq_len):
    return (jnp.ones((seq_len, LANES), F32), jnp.zeros((seq_len, LANES), F32), jnp.zeros((seq_len, LANES), F32))


def _pool_band():
    r = np.arange(BLOCK)[:, None]
    j = np.arange(BLOCK + 2 * POOL_HALO)[None, :] - POOL_HALO
    return np.stack([((j >= r - w // 2) & (j < r + w // 2)) for w in POOL_WINDOWS]).astype(np.float32)


def _band_bias():
    qi = np.tile(np.arange(BLOCK), GQA_GROUP)[:, None]
    ki = np.arange(BLOCK)[None, :]
    prev = np.where(ki - BLOCK >= qi - WINDOW, 0.0, NEG_BIG)
    nxt = np.where(ki + BLOCK <= qi + WINDOW, 0.0, NEG_BIG)
    return np.stack([prev, nxt]).astype(np.float32)


def _mix0_kernel(sinks_ref, x_ref, q_ref, pprev_ref, pcur_ref, pnext_ref, kvprev_ref, kvcur_ref, kvnext_ref,
                 kvctx_ref, mod_ref, band_ref, bias_ref, wpool_ref, pscale_ref, wout_ref, lng_ref, lnb_ref,
                 o_ref, cat_ref, *, tiles_per_seq, seq_len):
    tm = x_ref.shape[0]
    nb = tm // BLOCK
    assert nb >= 2
    n_ctx = kvctx_ref.shape[0]
    n_keys = 3 * BLOCK + n_ctx
    it = pl.program_id(0) % tiles_per_seq
    first = it == 0
    last = it == tiles_per_seq - 1

    halo_zero = jnp.zeros((POOL_HALO, POOL_DIM), BF16)
    pext = jnp.concatenate([jnp.where(first, halo_zero, pprev_ref[...]), pcur_ref[...],
                            jnp.where(last, halo_zero, pnext_ref[...])], axis=0)
    row = lax.broadcasted_iota(jnp.int32, (BLOCK, POOL_GDIM), 0)

    sg = [(s, g) for s in range(nb) for g in range(POOL_GROUPS)]
    wins = [jnp.dot(band_ref[g], pext[s * BLOCK:(s + 1) * BLOCK + 2 * POOL_HALO, g * POOL_GDIM:(g + 1) * POOL_GDIM],
                    preferred_element_type=F32) for s, g in sg]
    dlts = []
    for (s, g), win in zip(sg, wins):
        w = POOL_WINDOWS[g]
        t = it * tm + s * BLOCK + row
        cnt = jnp.minimum(t + w // 2, seq_len) - jnp.maximum(t - w // 2, 0)
        pc = pcur_ref[s * BLOCK:(s + 1) * BLOCK, g * POOL_GDIM:(g + 1) * POOL_GDIM].astype(F32)
        dlts.append((win / cnt.astype(F32) - pc).astype(BF16))
    for (s, g), dlt in zip(sg, dlts):
        cols = slice(g * POOL_GDIM, (g + 1) * POOL_GDIM)
        a = jnp.dot(dlt, wpool_ref[g], preferred_element_type=F32) * pscale_ref[:, cols]
        cat_ref[s * BLOCK:(s + 1) * BLOCK, cols] = a.astype(BF16)

    bias_prev = bias_ref[0]
    bias_next = bias_ref[1]
    bias_prev0 = jnp.where(first, NEG_BIG, bias_prev)
    bias_next1 = jnp.where(last, NEG_BIG, bias_next)
    low = lax.broadcasted_iota(jnp.int32, (BLOCK, LANES), 1) < HEAD_DIM
    ones_v = jnp.ones((n_keys, LANES), BF16)

    def scores(h, s):
        kcol = slice(h * LANES, (h + 1) * LANES)
        rows = slice(s * BLOCK, (s + 1) * BLOCK)
        qst = jnp.concatenate([q_ref[rows, (h * GQA_GROUP + g) * LANES:(h * GQA_GROUP + g + 1) * LANES]
                               for g in range(GQA_GROUP)], axis=0)
        kext = jnp.concatenate([kvprev_ref[:, kcol], kvcur_ref[:, kcol], kvnext_ref[:, kcol]], axis=0)
        kcat = jnp.concatenate([kext[s * BLOCK:(s + 3) * BLOCK], kvctx_ref[:, kcol]], axis=0)
        return lax.dot_general(qst, kcat, (((1,), (1,)), ((), ())), preferred_element_type=F32)

    def finish(h, s, sc):
        vcol = slice((N_KV_HEADS + h) * LANES, (N_KV_HEADS + h + 1) * LANES)
        rows = slice(s * BLOCK, (s + 1) * BLOCK)
        vext = jnp.concatenate([kvprev_ref[:, vcol], kvcur_ref[:, vcol], kvnext_ref[:, vcol]], axis=0)
        vcat = jnp.concatenate([jnp.concatenate([vext[s * BLOCK:(s + 3) * BLOCK], kvctx_ref[:, vcol]], axis=0),
                                ones_v], axis=1)
        sink = jnp.concatenate([jnp.full((BLOCK, 1), sinks_ref[h * GQA_GROUP + g] * LOG2E, F32)
                                for g in range(GQA_GROUP)], axis=0)
        bp = bias_prev0 if s == 0 else bias_prev
        bn = bias_next1 if s == nb - 1 else bias_next
        sc = jnp.concatenate([sc[:, :BLOCK] + bp, sc[:, BLOCK:2 * BLOCK], sc[:, 2 * BLOCK:3 * BLOCK] + bn,
                              sc[:, 3 * BLOCK:]], axis=1)
        m = jnp.maximum(jnp.max(sc, axis=-1, keepdims=True), sink)
        e = jnp.exp2(sc - m).astype(BF16)
        o2 = jnp.dot(e, vcat, preferred_element_type=F32)
        o = o2[:, :LANES] / (o2[:, LANES:] + jnp.exp2(sink - m))
        for j in range(GQA_GROUP // 2):
            pair = jnp.where(low, o[(2 * j) * BLOCK:(2 * j + 1) * BLOCK], o[(2 * j + 1) * BLOCK:(2 * j + 2) * BLOCK])
            c0 = POOL_DIM + (h * GQA_GROUP // 2 + j) * LANES
            cat_ref[rows, c0:c0 + LANES] = pair.astype(BF16)

    todo = [(h, s) for h in range(N_KV_HEADS) for s in range(nb)]
    sc_next = scores(*todo[0])
    for i, (h, s) in enumerate(todo):
        sc = sc_next
        if i + 1 < len(todo):
            sc_next = scores(*todo[i + 1])
        finish(h, s, sc)

    mix = jnp.dot(cat_ref[...], wout_ref[...], preferred_element_type=F32)
    z = ALPHA * x_ref[...] + mod_ref[2:3, :] * mix
    o_ref[...] = _ln(z) * lng_ref[...] + lnb_ref[...]


def _mix0(x, u, u_ctx, mod, sinks, w_pool, pool_scale, w_out, lng, lnb, *, seq_len, n_ctx, tm):
    n, d = x.shape
    nb = tm // BLOCK
    tiles_per_seq = seq_len // tm
    hb = tm // POOL_HALO
    q_blk, p_blk, kv_blk = 0, U_Q // U_P, (U_Q + U_P) // U_KV
    band = jnp.asarray(_pool_band(), BF16)
    bias = jnp.asarray(_band_bias(), F32)
    kernel = functools.partial(_mix0_kernel, tiles_per_seq=tiles_per_seq, seq_len=seq_len)
    vmem = 4 * tm * d * 4 + 4 * tm * U_COLS * 2 + 4 * d * d * 2 + 6 * GQA_GROUP * BLOCK * (3 * BLOCK + n_ctx) * 4 + (8 << 20)
    return pl.pallas_call(
        kernel,
        grid=(n // tm,),
        in_specs=[
            pl.BlockSpec(memory_space=pltpu.SMEM),
            pl.BlockSpec((tm, d), lambda i: (i, 0)),
            pl.BlockSpec((tm, U_Q), lambda i: (i, q_blk)),
            pl.BlockSpec((POOL_HALO, U_P), lambda i: (jnp.maximum(i * hb - 1, 0), p_blk)),
            pl.BlockSpec((tm, U_P), lambda i: (i, p_blk)),
            pl.BlockSpec((POOL_HALO, U_P), lambda i: (jnp.minimum((i + 1) * hb, n // POOL_HALO - 1), p_blk)),
            pl.BlockSpec((BLOCK, U_KV), lambda i: (jnp.maximum(i * nb - 1, 0), kv_blk)),
            pl.BlockSpec((tm, U_KV), lambda i: (i, kv_blk)),
            pl.BlockSpec((BLOCK, U_KV), lambda i: (jnp.minimum((i + 1) * nb, n // BLOCK - 1), kv_blk)),
            pl.BlockSpec((n_ctx, U_KV), lambda i: (i // tiles_per_seq, kv_blk)),
            pl.BlockSpec((None, 3, d), lambda i: (i // tiles_per_seq, 0, 0)),
            _resident(band.shape),
            _resident(bias.shape),
            _resident(w_pool.shape),
            _resident((1, POOL_DIM)),
            _resident(w_out.shape),
            _resident((1, d)),
            _resident((1, d)),
        ],
        out_specs=pl.BlockSpec((tm, d), lambda i: (i, 0)),
        out_shape=jax.ShapeDtypeStruct((n, d), F32),
        scratch_shapes=[pltpu.VMEM((tm, POOL_DIM + Q_DIM), BF16)],
        compiler_params=_params(vmem),
        name="mix0",
    )(sinks, x, u, u, u, u, u, u, u, u_ctx, mod, band, bias, w_pool, pool_scale, w_out, lng, lnb)


def _dft_tables(seq_len):
    t1n, t2n, c = seq_len // GRID_W, GRID_W, FOURIER_GDIM
    half = t1n // 2
    a1 = 2 * np.pi * (np.outer(np.arange(t1n), np.arange(t1n)) % t1n) / t1n
    m1 = np.concatenate([np.cos(a1[:half + 1]), np.sin(a1[1:half])], axis=0) / np.sqrt(t1n)

    t2 = np.arange(t2n)
    base = 2 * np.pi * (np.outer(t2, t2) % t2n) / t2n

    def amat(k1):
        ang = base + 2 * np.pi * k1 * t2[None, :] / seq_len
        return np.cos(ang) / np.sqrt(t2n), -np.sin(ang) / np.sqrt(t2n)

    zero = np.zeros((t2n, t2n))
    m2 = []
    for u in range(half):
        if u == 0:
            ar, ai = amat(0)
            br, bi = amat(half)
            m2.append(np.block([[ar, zero], [zero, br], [ai, zero], [zero, bi]]))
        else:
            ar, ai = amat(u)
            br, bi = amat(t1n - u)
            m2.append(np.block([[ar, ai], [br, -bi], [ai, -ar], [bi, br]]))
    m2 = np.stack(m2)
    a3 = 2 * np.pi * (np.outer(np.arange(c), np.arange(c)) % c) / c
    cs = np.concatenate([np.cos(a3), np.sin(a3)], axis=0) / np.sqrt(c)
    return tuple(jnp.asarray(t, F32).astype(BF16) for t in (m1, m2, cs))


def _dft1_kernel(x_hbm, mod_ref, m1_ref, z_hbm, xbuf, zbuf, in_sem, out_sem, *, nt2, steps_per_batch):
    n = pl.program_id(0)
    nsteps = pl.num_programs(0)
    slot = n % 2

    def where(step, i):
        return step // steps_per_batch, (step % steps_per_batch) * nt2 + i

    def load(step, sl, i):
        b, t2 = where(step, i)
        return pltpu.make_async_copy(x_hbm.at[b, :, t2, :], xbuf.at[sl, i], in_sem.at[sl])

    def store(step, sl, i):
        b, t2 = where(step, i)
        return pltpu.make_async_copy(zbuf.at[sl, i], z_hbm.at[b, :, t2, :], out_sem.at[sl])

    @pl.when(n == 0)
    def _():
        for i in range(nt2):
            load(n, slot, i).start()

    @pl.when(n + 1 < nsteps)
    def _():
        for i in range(nt2):
            load(n + 1, 1 - slot, i).start()

    @pl.when(n >= 2)
    def _():
        for i in range(nt2):
            store(n - 2, slot, i).wait()

    for i in range(nt2):
        load(n, slot, i).wait()
    hs = [_modulate_bf16(xbuf[slot, i], mod_ref) for i in range(nt2)]
    for i in range(nt2):
        zbuf[slot, i] = jnp.dot(m1_ref[...], hs[i], preferred_element_type=F32)
    for i in range(nt2):
        store(n, slot, i).start()

    @pl.when(n == nsteps - 1)
    def _():
        for i in range(nt2):
            store(n, slot, i).wait()

    @pl.when((n == nsteps - 1) & (n >= 1))
    def _():
        for i in range(nt2):
            store(n - 1, 1 - slot, i).wait()


def _dft23_kernel(x_hbm, zp_ref, zq_ref, m2_ref, mod_ref, cs_ref, wout_ref, lng_ref, lnb_ref, o_hbm,
                  xbuf, obuf, in_sem, out_sem, *, units, steps_per_batch, t1n):
    n = pl.program_id(0)
    nsteps = pl.num_programs(0)
    slot = n % 2
    t2n = zp_ref.shape[1]
    half = t1n // 2

    def where(step, g, side):
        u = (step % steps_per_batch) * units + g
        return step // steps_per_batch, (u if side == 0 else jnp.where(u == 0, half, t1n - u))

    def load(step, sl, g, side):
        b, k1 = where(step, g, side)
        return pltpu.make_async_copy(x_hbm.at[b, :, k1, :], xbuf.at[sl, g, side], in_sem.at[sl])

    def store(step, sl, g, side):
        b, k1 = where(step, g, side)
        return pltpu.make_async_copy(obuf.at[sl, g, side], o_hbm.at[b, :, k1, :], out_sem.at[sl])

    pairs = [(g, side) for g in range(units) for side in range(2)]

    @pl.when(n == 0)
    def _():
        for g, side in pairs:
            load(n, slot, g, side).start()

    @pl.when(n + 1 < nsteps)
    def _():
        for g, side in pairs:
            load(n + 1, 1 - slot, g, side).start()

    @pl.when(n >= 2)
    def _():
        for g, side in pairs:
            store(n - 2, slot, g, side).wait()

    for g, side in pairs:
        load(n, slot, g, side).wait()

    ngroups = min(units, 4)
    per = units // ngroups
    groups = [range(i * per, (i + 1) * per) for i in range(ngroups)]

    def stage2(gs):
        y = [jnp.dot(m2_ref[g], jnp.concatenate([zp_ref[g], zq_ref[g]], axis=0).astype(BF16),
                     preferred_element_type=F32) for g in gs]
        return (jnp.concatenate([t[:2 * t2n] for t in y], axis=0).astype(BF16),
                jnp.concatenate([t[2 * t2n:] for t in y], axis=0).astype(BF16))

    def project(yr, yi):
        parts = []
        for c in range(FOURIER_GROUPS):
            cols = slice(c * FOURIER_GDIM, (c + 1) * FOURIER_GDIM)
            parts.append(jnp.dot(jnp.concatenate([yr[:, cols], yi[:, cols]], axis=1), cs_ref[...],
                                 preferred_element_type=F32))
        f = jnp.concatenate(parts, axis=1).astype(BF16)
        return jnp.dot(f, wout_ref[...], preferred_element_type=F32)

    def epilogue(gs, mix):
        for g in gs:
            for side in range(2):
                r0 = (2 * (g - gs[0]) + side) * t2n
                z = ALPHA * xbuf[slot, g, side] + mod_ref[2:3, :] * mix[r0:r0 + t2n]
                obuf[slot, g, side] = _ln(z) * lng_ref[...] + lnb_ref[...]

    y_next = stage2(groups[0])
    pending = None
    for i, gs in enumerate(groups):
        y = y_next
        if i + 1 < ngroups:
            y_next = stage2(groups[i + 1])
        if pending is not None:
            epilogue(*pending)
        pending = (gs, project(*y))
    epilogue(*pending)

    for g, side in pairs:
        store(n, slot, g, side).start()

    @pl.when(n == nsteps - 1)
    def _():
        for g, side in pairs:
            store(n, slot, g, side).wait()

    @pl.when((n == nsteps - 1) & (n >= 1))
    def _():
        for g, side in pairs:
            store(n - 1, 1 - slot, g, side).wait()


def _fourier_mix(x, mod, w_out, lng, lnb, *, batch, seq_len, nt2=4, units=8):
    n, d = x.shape
    t1n, t2n = seq_len // GRID_W, GRID_W
    half = t1n // 2
    assert t2n % nt2 == 0 and half % units == 0 and units % min(units, 4) == 0
    m1, m2, cs = _dft_tables(seq_len)
    any_spec = pl.BlockSpec(memory_space=pl.ANY)
    arb = lambda vmem: pltpu.CompilerParams(dimension_semantics=("arbitrary",), vmem_limit_bytes=int(vmem))

    spb1 = t2n // nt2
    z = pl.pallas_call(
        functools.partial(_dft1_kernel, nt2=nt2, steps_per_batch=spb1),
        grid=(batch * spb1,),
        in_specs=[
            any_spec,
            pl.BlockSpec((None, 3, d), lambda i: (i // spb1, 0, 0)),
            _resident(m1.shape),
        ],
        out_specs=any_spec,
        out_shape=jax.ShapeDtypeStruct((batch, t1n, t2n, d), F32),
        scratch_shapes=[pltpu.VMEM((2, nt2, t1n, d), F32), pltpu.VMEM((2, nt2, t1n, d), F32),
                        pltpu.SemaphoreType.DMA((2,)), pltpu.SemaphoreType.DMA((2,))],
        compiler_params=arb(4 * nt2 * t1n * d * 4 + 6 * t1n * d * 4 + (8 << 20)),
        name="dft1",
    )(x.reshape(batch, t1n, t2n, d), mod, m1)

    spb2 = half // units
    tok = units * 2 * t2n
    zblk = lambda off: pl.BlockSpec((None, units, t2n, d), lambda i: (i // spb2, off + i % spb2, 0, 0))
    out = pl.pallas_call(
        functools.partial(_dft23_kernel, units=units, steps_per_batch=spb2, t1n=t1n),
        grid=(batch * spb2,),
        in_specs=[
            any_spec,
            zblk(0),
            zblk(spb2),
            pl.BlockSpec((units, 4 * t2n, 2 * t2n), lambda i: (i % spb2, 0, 0)),
            pl.BlockSpec((None, 3, d), lambda i: (i // spb2, 0, 0)),
            _resident(cs.shape),
            _resident(w_out.shape),
            _resident((1, d)),
            _resident((1, d)),
        ],
        out_specs=any_spec,
        out_shape=jax.ShapeDtypeStruct((batch, t2n, t1n, d), F32),
        scratch_shapes=[pltpu.VMEM((2, units, 2, t2n, d), F32), pltpu.VMEM((2, units, 2, t2n, d), F32),
                        pltpu.SemaphoreType.DMA((2,)), pltpu.SemaphoreType.DMA((2,))],
        compiler_params=arb(4 * tok * d * 4 + 4 * tok * d * 4 + 2 * d * d * 2 + 10 * tok * d * 4 + (8 << 20)),
        name="dft23",
    )(x.reshape(batch, t2n, t1n, d), z, z, m2, mod, cs, w_out, lng, lnb)
    return out.reshape(n, d)


FFN_TM = 2048
FFN_SUB = 256
PROJ_TM = 1024
PROJ_SUB = 256
MIX_TM = 1024


def kernel(x, c, ctx, c_ctx, ada_w, ada_b, ln_g, ln_b, ffn1_w1, ffn1_w3, ffn1_w2, ffn2_w1, ffn2_w3, ffn2_w2,
           mix_w_in, pool_w, pool_scale, attn_sinks, mix_w_out, fourier_w_out):
    batch, seq_len, d = x.shape
    n_ctx = ctx.shape[1]
    assert d == D_MODEL and seq_len % GRID_W == 0
    assert seq_len % FFN_TM == 0 and seq_len % PROJ_TM == 0 and seq_len % MIX_TM == 0
    assert n_ctx % BLOCK == 0

    c_rows = jnp.zeros((2 * 8, d), F32).at[:batch].set(c).at[batch].set(c_ctx)
    mod_all = _ada(c_rows, ada_w, ada_b).reshape(DEPTH, 2 * 8, N_MOD, d)

    xf = x.reshape(batch * seq_len, d)
    cf = ctx.reshape(batch * n_ctx, d)
    bf = lambda w: w.astype(BF16)
    ffn1 = (bf(ffn1_w1), bf(ffn1_w3), bf(ffn1_w2))
    ffn2 = (bf(ffn2_w1), bf(ffn2_w3), bf(ffn2_w2))
    for l in range(DEPTH):
        mod = mod_all[l, :batch]
        lng = ln_g[l][:, None, :]
        lnb = ln_b[l][:, None, :]
        xf = _ffn(xf, mod[:, 0:3], lng[0], lnb[0], *ffn1, layer=l, seq_len=seq_len, tm=FFN_TM, sub=FFN_SUB)
        if l % 2 == 0:
            e = l // 2
            mod_c = mod_all[l, batch:batch + 1]
            cf = _ffn(cf, mod_c[:, 0:3], lng[0], lnb[0], *ffn1, layer=l, seq_len=n_ctx, tm=n_ctx, sub=n_ctx)
            w_in = bf(mix_w_in[e])
            u = _proj(xf, mod[:, 3:6], w_in, _rope_tables(seq_len), seq_len=seq_len, tm=PROJ_TM, sub=PROJ_SUB)
            u_ctx = _proj(cf, mod_c[:, 3:6], w_in, _identity_rope_tables(n_ctx), seq_len=n_ctx, tm=n_ctx, sub=n_ctx)
            xf = _mix0(xf, u, u_ctx, mod[:, 3:6], attn_sinks[e], bf(pool_w[e]), pool_scale[e][None, :],
                       bf(mix_w_out[e]), lng[1], lnb[1], seq_len=seq_len, n_ctx=n_ctx, tm=MIX_TM)
        else:
            xf = _fourier_mix(xf, mod[:, 3:6], bf(fourier_w_out[l // 2]), lng[1], lnb[1],
                              batch=batch, seq_len=seq_len)
        xf = _ffn(xf, mod[:, 6:9], lng[2], lnb[2], *ffn2, layer=l, seq_len=seq_len, tm=FFN_TM, sub=FFN_SUB)
    return xf.reshape(batch, seq_len, d)
```

```python
import functools

import numpy as np
import jax
import jax.numpy as jnp
from jax import lax
from jax.experimental import pallas as pl
from jax.experimental.pallas import tpu as pltpu

F32 = jnp.float32
BF16 = jnp.bfloat16

D_MODEL = 1024
DEPTH = 2
GRID_W = 64
POOL_GROUPS = 4
POOL_WINDOWS = (2, 4, 8, 16)
POOL_DIM = D_MODEL // 2
POOL_GDIM = POOL_DIM // POOL_GROUPS
HEAD_DIM = 64
N_Q_HEADS = (D_MODEL // 2) // HEAD_DIM
N_KV_HEADS = 2
GQA_GROUP = N_Q_HEADS // N_KV_HEADS
Q_DIM = N_Q_HEADS * HEAD_DIM
KV_DIM = N_KV_HEADS * HEAD_DIM
WINDOW = 128
BLOCK = 128
ROPE_BASE = 10000.0
ROPE_HALF = HEAD_DIM // 2
FOURIER_GROUPS = 4
FOURIER_GDIM = D_MODEL // FOURIER_GROUPS
N_MOD = 9
ALPHA = (2 * DEPTH) ** 0.25
LN_EPS = 1e-5
ATTN_SCALE = HEAD_DIM ** -0.5
LOG2E = 1.4426950408889634

LANES = 128
BF16_SUBLANES = 16
V7X_VMEM_BYTES = 64 * 1024 * 1024

U_Q = N_Q_HEADS * LANES
U_P = POOL_DIM
U_KV = 4 * LANES
U_COLS = U_Q + U_P + U_KV
POOL_HALO = BF16_SUBLANES
NEG_BIG = -1e30


def _ln(x):
    mu = jnp.mean(x, axis=-1, keepdims=True)
    xc = x - mu
    var = jnp.mean(xc * xc, axis=-1, keepdims=True)
    return xc * lax.rsqrt(var + LN_EPS)


def _modulate_bf16(x, mod_ref):
    return (_ln(x) * (1.0 + mod_ref[1:2, :]) + mod_ref[0:1, :]).astype(BF16)


def _params(vmem_bytes):
    return pltpu.CompilerParams(dimension_semantics=("parallel",), vmem_limit_bytes=int(vmem_bytes))


def _params2(vmem_bytes):
    return pltpu.CompilerParams(dimension_semantics=("parallel", "parallel"), vmem_limit_bytes=int(vmem_bytes))


def _resident(shape):
    nd = len(shape)
    return pl.BlockSpec(shape, lambda *_: (0,) * nd, pipeline_mode=pl.Buffered(1))


def _ada_kernel(c_ref, w_ref, b_ref, o_ref):
    c = c_ref[...]
    s = c * jax.nn.sigmoid(c)
    o_ref[...] = jnp.dot(s, w_ref[...], preferred_element_type=F32) + b_ref[...]


def _ada(c_rows, ada_w, ada_b):
    r, d = c_rows.shape
    depth, _, n = ada_w.shape
    tn = n // 8
    return pl.pallas_call(
        _ada_kernel,
        grid=(depth, n // tn),
        in_specs=[
            pl.BlockSpec((r, d), lambda l, j: (0, 0)),
            pl.BlockSpec((None, d, tn), lambda l, j: (l, 0, j)),
            pl.BlockSpec((None, 1, tn), lambda l, j: (l, 0, j)),
        ],
        out_specs=pl.BlockSpec((None, r, tn), lambda l, j: (l, 0, j)),
        out_shape=jax.ShapeDtypeStruct((depth, r, n), F32),
        compiler_params=_params2(2 * d * tn * 4 + (4 << 20)),
        name="ada",
    )(c_rows, ada_w, ada_b.reshape(depth, 1, n))


def _ffn_kernel(x_ref, mod_ref, lng_ref, lnb_ref, w1_ref, w3_ref, w2_ref, o_ref, *, sub):
    for r in range(x_ref.shape[0] // sub):
        rows = slice(r * sub, (r + 1) * sub)
        x = x_ref[rows, :]
        h = _modulate_bf16(x, mod_ref)
        a = jnp.dot(h, w1_ref[...], preferred_element_type=F32)
        b = jnp.dot(h, w3_ref[...], preferred_element_type=F32)
        act = (a * jax.nn.sigmoid(a) * b).astype(BF16)
        y = jnp.dot(act, w2_ref[...], preferred_element_type=F32)
        z = ALPHA * x + (0.5 * mod_ref[2:3, :]) * y
        o_ref[rows, :] = _ln(z) * lng_ref[...] + lnb_ref[...]


def _ffn(x, mod, lng, lnb, w1, w3, w2, *, layer, seq_len, tm, sub):
    n, d = x.shape
    f = w1.shape[2]
    layer_block = lambda rows, cols: pl.BlockSpec((None, rows, cols), lambda i: (layer, 0, 0),
                                                  pipeline_mode=pl.Buffered(1))
    tiles_per_seq = seq_len // tm
    mod_idx = (lambda i: (0, 0, 0)) if mod.shape[0] == 1 else (lambda i: (i // tiles_per_seq, 0, 0))
    vmem = min(4 * tm * d * 4 + 3 * d * f * 2 + (tm // sub) * (9 << 17) + (1 << 20), V7X_VMEM_BYTES)
    return pl.pallas_call(
        functools.partial(_ffn_kernel, sub=sub),
        grid=(n // tm,),
        in_specs=[
            pl.BlockSpec((tm, d), lambda i: (i, 0)),
            pl.BlockSpec((None, 3, d), mod_idx),
            _resident((1, d)),
            _resident((1, d)),
            layer_block(d, f),
            layer_block(d, f),
            layer_block(f, d),
        ],
        out_specs=pl.BlockSpec((tm, d), lambda i: (i, 0)),
        out_shape=jax.ShapeDtypeStruct((n, d), F32),
        compiler_params=_params(vmem),
        name="ffn",
    )(x, mod, lng, lnb, w1, w3, w2)


def _proj_kernel(x_ref, mod_ref, w_ref, cos_ref, sa_ref, sb_ref, o_ref, *, sub):
    low = lax.broadcasted_iota(jnp.int32, (sub, LANES), 1) < HEAD_DIM
    for r in range(x_ref.shape[0] // sub):
        rows = slice(r * sub, (r + 1) * sub)
        h = _modulate_bf16(x_ref[rows, :], mod_ref)
        u = jnp.dot(h, w_ref[...], preferred_element_type=F32)
        cos, sa, sb = cos_ref[rows, :], sa_ref[rows, :], sb_ref[rows, :]

        def rope(t):
            return t * cos + pltpu.roll(t, LANES - ROPE_HALF // 2, 1) * sa + pltpu.roll(t, ROPE_HALF // 2, 1) * sb

        for c in range(Q_DIM // LANES):
            qc = rope(u[:, POOL_DIM + c * LANES:POOL_DIM + (c + 1) * LANES]) * (ATTN_SCALE * LOG2E)
            o_ref[rows, (2 * c) * LANES:(2 * c + 1) * LANES] = jnp.where(low, qc, 0.0).astype(BF16)
            o_ref[rows, (2 * c + 1) * LANES:(2 * c + 2) * LANES] = jnp.where(low, 0.0, qc).astype(BF16)
        o_ref[rows, U_Q:U_Q + U_P] = u[:, :POOL_DIM].astype(BF16)
        k = rope(u[:, POOL_DIM + Q_DIM:POOL_DIM + Q_DIM + KV_DIM])
        v = u[:, POOL_DIM + Q_DIM + KV_DIM:]
        base = U_Q + U_P
        for j, t in enumerate((k, v)):
            ts = pltpu.roll(t, HEAD_DIM, 1)
            o_ref[rows, base + (2 * j) * LANES:base + (2 * j + 1) * LANES] = jnp.where(low, t, ts).astype(BF16)
            o_ref[rows, base + (2 * j + 1) * LANES:base + (2 * j + 2) * LANES] = jnp.where(low, ts, t).astype(BF16)


def _proj(x, mod, w_in, rope_tabs, *, seq_len, tm, sub):
    n, d = x.shape
    tiles_per_seq = seq_len // tm
    mod_idx = (lambda i: (0, 0, 0)) if mod.shape[0] == 1 else (lambda i: (i // tiles_per_seq, 0, 0))
    tab_spec = pl.BlockSpec((tm, LANES), lambda i: (i % tiles_per_seq, 0))
    vmem = 2 * tm * d * 4 + 2 * tm * U_COLS * 2 + 2 * d * w_in.shape[1] * 2 + tm * w_in.shape[1] * 8 + (8 << 20)
    return pl.pallas_call(
        functools.partial(_proj_kernel, sub=sub),
        grid=(n // tm,),
        in_specs=[
            pl.BlockSpec((tm, d), lambda i: (i, 0)),
            pl.BlockSpec((None, 3, d), mod_idx),
            _resident(w_in.shape),
            tab_spec, tab_spec, tab_spec,
        ],
        out_specs=pl.BlockSpec((tm, U_COLS), lambda i: (i, 0)),
        out_shape=jax.ShapeDtypeStruct((n, U_COLS), BF16),
        compiler_params=_params(vmem),
        name="proj",
    )(x, mod, w_in, *rope_tabs)


def _rope_tables(seq_len):
    rows = seq_len // GRID_W
    row = jnp.repeat(jnp.arange(rows), GRID_W).astype(F32)
    col = jnp.tile(jnp.arange(GRID_W), rows).astype(F32)
    freqs = ROPE_BASE ** (-jnp.arange(0, ROPE_HALF, 2, dtype=F32) / ROPE_HALF)
    ang_row = row[:, None] * freqs[None]
    ang_col = col[:, None] * freqs[None]
    cr, sr, cc, sc = jnp.cos(ang_row), jnp.sin(ang_row), jnp.cos(ang_col), jnp.sin(ang_col)
    z = jnp.zeros_like(sr)
    cos = jnp.concatenate([cr, cr, cc, cc], axis=-1)
    sa = jnp.concatenate([-sr, z, -sc, z], axis=-1)
    sb = jnp.concatenate([z, sr, z, sc], axis=-1)
    return tuple(jnp.tile(t, (1, LANES // HEAD_DIM)) for t in (cos, sa, sb))


def _identity_rope_tables(seq_len):
    return (jnp.ones((seq_len, LANES), F32), jnp.zeros((seq_len, LANES), F32), jnp.zeros((seq_len, LANES), F32))


def _pool_band():
    r = np.arange(BLOCK)[:, None]
    j = np.arange(BLOCK + 2 * POOL_HALO)[None, :] - POOL_HALO
    return np.stack([((j >= r - w // 2) & (j < r + w // 2)) for w in POOL_WINDOWS]).astype(np.float32)


def _band_bias():
    qi = np.tile(np.arange(BLOCK), GQA_GROUP)[:, None]
    ki = np.arange(BLOCK)[None, :]
    prev = np.where(ki - BLOCK >= qi - WINDOW, 0.0, NEG_BIG)
    nxt = np.where(ki + BLOCK <= qi + WINDOW, 0.0, NEG_BIG)
    return np.stack([prev, nxt]).astype(np.float32)


def _mix0_kernel(sinks_ref, x_ref, q_ref, pprev_ref, pcur_ref, pnext_ref, kvprev_ref, kvcur_ref, kvnext_ref,
                 kvctx_ref, mod_ref, band_ref, bias_ref, wpool_ref, pscale_ref, wout_ref, lng_ref, lnb_ref,
                 o_ref, cat_ref, *, tiles_per_seq, seq_len):
    tm = x_ref.shape[0]
    nb = tm // BLOCK
    assert nb >= 2
    n_ctx = kvctx_ref.shape[0]
    n_keys = 3 * BLOCK + n_ctx
    it = pl.program_id(0) % tiles_per_seq
    first = it == 0
    last = it == tiles_per_seq - 1

    halo_zero = jnp.zeros((POOL_HALO, POOL_DIM), BF16)
    pext = jnp.concatenate([jnp.where(first, halo_zero, pprev_ref[...]), pcur_ref[...],
                            jnp.where(last, halo_zero, pnext_ref[...])], axis=0)
    row = lax.broadcasted_iota(jnp.int32, (BLOCK, POOL_GDIM), 0)

    sg = [(s, g) for s in range(nb) for g in range(POOL_GROUPS)]
    wins = [jnp.dot(band_ref[g], pext[s * BLOCK:(s + 1) * BLOCK + 2 * POOL_HALO, g * POOL_GDIM:(g + 1) * POOL_GDIM],
                    preferred_element_type=F32) for s, g in sg]
    dlts = []
    for (s, g), win in zip(sg, wins):
        w = POOL_WINDOWS[g]
        t = it * tm + s * BLOCK + row
        cnt = jnp.minimum(t + w // 2, seq_len) - jnp.maximum(t - w // 2, 0)
        pc = pcur_ref[s * BLOCK:(s + 1) * BLOCK, g * POOL_GDIM:(g + 1) * POOL_GDIM].astype(F32)
        dlts.append((win / cnt.astype(F32) - pc).astype(BF16))
    for (s, g), dlt in zip(sg, dlts):
        cols = slice(g * POOL_GDIM, (g + 1) * POOL_GDIM)
        a = jnp.dot(dlt, wpool_ref[g], preferred_element_type=F32) * pscale_ref[:, cols]
        cat_ref[s * BLOCK:(s + 1) * BLOCK, cols] = a.astype(BF16)

    bias_prev = bias_ref[0]
    bias_next = bias_ref[1]
    bias_prev0 = jnp.where(first, NEG_BIG, bias_prev)
    bias_next1 = jnp.where(last, NEG_BIG, bias_next)
    low = lax.broadcasted_iota(jnp.int32, (BLOCK, LANES), 1) < HEAD_DIM
    ones_v = jnp.ones((n_keys, LANES), BF16)

    def scores(h, s):
        kcol = slice(h * LANES, (h + 1) * LANES)
        rows = slice(s * BLOCK, (s + 1) * BLOCK)
        qst = jnp.concatenate([q_ref[rows, (h * GQA_GROUP + g) * LANES:(h * GQA_GROUP + g + 1) * LANES]
                               for g in range(GQA_GROUP)], axis=0)
        kext = jnp.concatenate([kvprev_ref[:, kcol], kvcur_ref[:, kcol], kvnext_ref[:, kcol]], axis=0)
        kcat = jnp.concatenate([kext[s * BLOCK:(s + 3) * BLOCK], kvctx_ref[:, kcol]], axis=0)
        return lax.dot_general(qst, kcat, (((1,), (1,)), ((), ())), preferred_element_type=F32)

    def finish(h, s, sc):
        vcol = slice((N_KV_HEADS + h) * LANES, (N_KV_HEADS + h + 1) * LANES)
        rows = slice(s * BLOCK, (s + 1) * BLOCK)
        vext = jnp.concatenate([kvprev_ref[:, vcol], kvcur_ref[:, vcol], kvnext_ref[:, vcol]], axis=0)
        vcat = jnp.concatenate([jnp.concatenate([vext[s * BLOCK:(s + 3) * BLOCK], kvctx_ref[:, vcol]], axis=0),
                                ones_v], axis=1)
        sink = jnp.concatenate([jnp.full((BLOCK, 1), sinks_ref[h * GQA_GROUP + g] * LOG2E, F32)
                                for g in range(GQA_GROUP)], axis=0)
        bp = bias_prev0 if s == 0 else bias_prev
        bn = bias_next1 if s == nb - 1 else bias_next
        sc = jnp.concatenate([sc[:, :BLOCK] + bp, sc[:, BLOCK:2 * BLOCK], sc[:, 2 * BLOCK:3 * BLOCK] + bn,
                              sc[:, 3 * BLOCK:]], axis=1)
        m = jnp.maximum(jnp.max(sc, axis=-1, keepdims=True), sink)
        e = jnp.exp2(sc - m).astype(BF16)
        o2 = jnp.dot(e, vcat, preferred_element_type=F32)
        o = o2[:, :LANES] / (o2[:, LANES:] + jnp.exp2(sink - m))
        for j in range(GQA_GROUP // 2):
            pair = jnp.where(low, o[(2 * j) * BLOCK:(2 * j + 1) * BLOCK], o[(2 * j + 1) * BLOCK:(2 * j + 2) * BLOCK])
            c0 = POOL_DIM + (h * GQA_GROUP // 2 + j) * LANES
            cat_ref[rows, c0:c0 + LANES] = pair.astype(BF16)

    todo = [(h, s) for h in range(N_KV_HEADS) for s in range(nb)]
    sc_next = scores(*todo[0])
    for i, (h, s) in enumerate(todo):
        sc = sc_next
        if i + 1 < len(todo):
            sc_next = scores(*todo[i + 1])
        finish(h, s, sc)

    mix = jnp.dot(cat_ref[...], wout_ref[...], preferred_element_type=F32)
    z = ALPHA * x_ref[...] + mod_ref[2:3, :] * mix
    o_ref[...] = _ln(z) * lng_ref[...] + lnb_ref[...]


def _mix0(x, u, u_ctx, mod, sinks, w_pool, pool_scale, w_out, lng, lnb, *, seq_len, n_ctx, tm):
    n, d = x.shape
    nb = tm // BLOCK
    tiles_per_seq = seq_len // tm
    hb = tm // POOL_HALO
    q_blk, p_blk, kv_blk = 0, U_Q // U_P, (U_Q + U_P) // U_KV
    band = jnp.asarray(_pool_band(), BF16)
    bias = jnp.asarray(_band_bias(), F32)
    kernel = functools.partial(_mix0_kernel, tiles_per_seq=tiles_per_seq, seq_len=seq_len)
    vmem = 4 * tm * d * 4 + 4 * tm * U_COLS * 2 + 4 * d * d * 2 + 6 * GQA_GROUP * BLOCK * (3 * BLOCK + n_ctx) * 4 + (8 << 20)
    return pl.pallas_call(
        kernel,
        grid=(n // tm,),
        in_specs=[
            pl.BlockSpec(memory_space=pltpu.SMEM),
            pl.BlockSpec((tm, d), lambda i: (i, 0)),
            pl.BlockSpec((tm, U_Q), lambda i: (i, q_blk)),
            pl.BlockSpec((POOL_HALO, U_P), lambda i: (jnp.maximum(i * hb - 1, 0), p_blk)),
            pl.BlockSpec((tm, U_P), lambda i: (i, p_blk)),
            pl.BlockSpec((POOL_HALO, U_P), lambda i: (jnp.minimum((i + 1) * hb, n // POOL_HALO - 1), p_blk)),
            pl.BlockSpec((BLOCK, U_KV), lambda i: (jnp.maximum(i * nb - 1, 0), kv_blk)),
            pl.BlockSpec((tm, U_KV), lambda i: (i, kv_blk)),
            pl.BlockSpec((BLOCK, U_KV), lambda i: (jnp.minimum((i + 1) * nb, n // BLOCK - 1), kv_blk)),
            pl.BlockSpec((n_ctx, U_KV), lambda i: (i // tiles_per_seq, kv_blk)),
            pl.BlockSpec((None, 3, d), lambda i: (i // tiles_per_seq, 0, 0)),
            _resident(band.shape),
            _resident(bias.shape),
            _resident(w_pool.shape),
            _resident((1, POOL_DIM)),
            _resident(w_out.shape),
            _resident((1, d)),
            _resident((1, d)),
        ],
        out_specs=pl.BlockSpec((tm, d), lambda i: (i, 0)),
        out_shape=jax.ShapeDtypeStruct((n, d), F32),
        scratch_shapes=[pltpu.VMEM((tm, POOL_DIM + Q_DIM), BF16)],
        compiler_params=_params(vmem),
        name="mix0",
    )(sinks, x, u, u, u, u, u, u, u, u_ctx, mod, band, bias, w_pool, pool_scale, w_out, lng, lnb)


def _dft_tables(seq_len):
    t1n, t2n, c = seq_len // GRID_W, GRID_W, FOURIER_GDIM
    half = t1n // 2
    a1 = 2 * np.pi * (np.outer(np.arange(t1n), np.arange(t1n)) % t1n) / t1n
    m1 = np.concatenate([np.cos(a1[:half + 1]), np.sin(a1[1:half])], axis=0) / np.sqrt(t1n)

    t2 = np.arange(t2n)
    base = 2 * np.pi * (np.outer(t2, t2) % t2n) / t2n

    def amat(k1):
        ang = base + 2 * np.pi * k1 * t2[None, :] / seq_len
        return np.cos(ang) / np.sqrt(t2n), -np.sin(ang) / np.sqrt(t2n)

    zero = np.zeros((t2n, t2n))
    m2 = []
    for u in range(half):
        if u == 0:
            ar, ai = amat(0)
            br, bi = amat(half)
            m2.append(np.block([[ar, zero], [zero, br], [ai, zero], [zero, bi]]))
        else:
            ar, ai = amat(u)
            br, bi = amat(t1n - u)
            m2.append(np.block([[ar, ai], [br, -bi], [ai, -ar], [bi, br]]))
    m2 = np.stack(m2)
    a3 = 2 * np.pi * (np.outer(np.arange(c), np.arange(c)) % c) / c
    cs = np.concatenate([np.cos(a3), np.sin(a3)], axis=0) / np.sqrt(c)
    return tuple(jnp.asarray(t, F32).astype(BF16) for t in (m1, m2, cs))


def _dft1_kernel(x_hbm, mod_ref, m1_ref, z_hbm, xbuf, zbuf, in_sem, out_sem, *, nt2, steps_per_batch):
    n = pl.program_id(0)
    nsteps = pl.num_programs(0)
    slot = n % 2

    def where(step, i):
        return step // steps_per_batch, (step % steps_per_batch) * nt2 + i

    def load(step, sl, i):
        b, t2 = where(step, i)
        return pltpu.make_async_copy(x_hbm.at[b, :, t2, :], xbuf.at[sl, i], in_sem.at[sl])

    def store(step, sl, i):
        b, t2 = where(step, i)
        return pltpu.make_async_copy(zbuf.at[sl, i], z_hbm.at[b, :, t2, :], out_sem.at[sl])

    @pl.when(n == 0)
    def _():
        for i in range(nt2):
            load(n, slot, i).start()

    @pl.when(n + 1 < nsteps)
    def _():
        for i in range(nt2):
            load(n + 1, 1 - slot, i).start()

    @pl.when(n >= 2)
    def _():
        for i in range(nt2):
            store(n - 2, slot, i).wait()

    for i in range(nt2):
        load(n, slot, i).wait()
    hs = [_modulate_bf16(xbuf[slot, i], mod_ref) for i in range(nt2)]
    for i in range(nt2):
        zbuf[slot, i] = jnp.dot(m1_ref[...], hs[i], preferred_element_type=F32)
    for i in range(nt2):
        store(n, slot, i).start()

    @pl.when(n == nsteps - 1)
    def _():
        for i in range(nt2):
            store(n, slot, i).wait()

    @pl.when((n == nsteps - 1) & (n >= 1))
    def _():
        for i in range(nt2):
            store(n - 1, 1 - slot, i).wait()


def _dft23_kernel(x_hbm, zp_ref, zq_ref, m2_ref, mod_ref, cs_ref, wout_ref, lng_ref, lnb_ref, o_hbm,
                  xbuf, obuf, in_sem, out_sem, *, units, steps_per_batch, t1n):
    n = pl.program_id(0)
    nsteps = pl.num_programs(0)
    slot = n % 2
    t2n = zp_ref.shape[1]
    half = t1n // 2

    def where(step, g, side):
        u = (step % steps_per_batch) * units + g
        return step // steps_per_batch, (u if side == 0 else jnp.where(u == 0, half, t1n - u))

    def load(step, sl, g, side):
        b, k1 = where(step, g, side)
        return pltpu.make_async_copy(x_hbm.at[b, :, k1, :], xbuf.at[sl, g, side], in_sem.at[sl])

    def store(step, sl, g, side):
        b, k1 = where(step, g, side)
        return pltpu.make_async_copy(obuf.at[sl, g, side], o_hbm.at[b, :, k1, :], out_sem.at[sl])

    pairs = [(g, side) for g in range(units) for side in range(2)]

    @pl.when(n == 0)
    def _():
        for g, side in pairs:
            load(n, slot, g, side).start()

    @pl.when(n + 1 < nsteps)
    def _():
        for g, side in pairs:
            load(n + 1, 1 - slot, g, side).start()

    @pl.when(n >= 2)
    def _():
        for g, side in pairs:
            store(n - 2, slot, g, side).wait()

    for g, side in pairs:
        load(n, slot, g, side).wait()

    ngroups = min(units, 4)
    per = units // ngroups
    groups = [range(i * per, (i + 1) * per) for i in range(ngroups)]

    def stage2(gs):
        y = [jnp.dot(m2_ref[g], jnp.concatenate([zp_ref[g], zq_ref[g]], axis=0).astype(BF16),
                     preferred_element_type=F32) for g in gs]
        return (jnp.concatenate([t[:2 * t2n] for t in y], axis=0).astype(BF16),
                jnp.concatenate([t[2 * t2n:] for t in y], axis=0).astype(BF16))

    def project(yr, yi):
        parts = []
        for c in range(FOURIER_GROUPS):
            cols = slice(c * FOURIER_GDIM, (c + 1) * FOURIER_GDIM)
            parts.append(jnp.dot(jnp.concatenate([yr[:, cols], yi[:, cols]], axis=1), cs_ref[...],
                                 preferred_element_type=F32))
        f = jnp.concatenate(parts, axis=1).astype(BF16)
        return jnp.dot(f, wout_ref[...], preferred_element_type=F32)

    def epilogue(gs, mix):
        for g in gs:
            for side in range(2):
                r0 = (2 * (g - gs[0]) + side) * t2n
                z = ALPHA * xbuf[slot, g, side] + mod_ref[2:3, :] * mix[r0:r0 + t2n]
                obuf[slot, g, side] = _ln(z) * lng_ref[...] + lnb_ref[...]

    y_next = stage2(groups[0])
    pending = None
    for i, gs in enumerate(groups):
        y = y_next
        if i + 1 < ngroups:
            y_next = stage2(groups[i + 1])
        if pending is not None:
            epilogue(*pending)
        pending = (gs, project(*y))
    epilogue(*pending)

    for g, side in pairs:
        store(n, slot, g, side).start()

    @pl.when(n == nsteps - 1)
    def _():
        for g, side in pairs:
            store(n, slot, g, side).wait()

    @pl.when((n == nsteps - 1) & (n >= 1))
    def _():
        for g, side in pairs:
            store(n - 1, 1 - slot, g, side).wait()


def _fourier_mix(x, mod, w_out, lng, lnb, *, batch, seq_len, nt2=4, units=8):
    n, d = x.shape
    t1n, t2n = seq_len // GRID_W, GRID_W
    half = t1n // 2
    assert t2n % nt2 == 0 and half % units == 0 and units % min(units, 4) == 0
    m1, m2, cs = _dft_tables(seq_len)
    any_spec = pl.BlockSpec(memory_space=pl.ANY)
    arb = lambda vmem: pltpu.CompilerParams(dimension_semantics=("arbitrary",), vmem_limit_bytes=int(vmem))

    spb1 = t2n // nt2
    z = pl.pallas_call(
        functools.partial(_dft1_kernel, nt2=nt2, steps_per_batch=spb1),
        grid=(batch * spb1,),
        in_specs=[
            any_spec,
            pl.BlockSpec((None, 3, d), lambda i: (i // spb1, 0, 0)),
            _resident(m1.shape),
        ],
        out_specs=any_spec,
        out_shape=jax.ShapeDtypeStruct((batch, t1n, t2n, d), F32),
        scratch_shapes=[pltpu.VMEM((2, nt2, t1n, d), F32), pltpu.VMEM((2, nt2, t1n, d), F32),
                        pltpu.SemaphoreType.DMA((2,)), pltpu.SemaphoreType.DMA((2,))],
        compiler_params=arb(4 * nt2 * t1n * d * 4 + 6 * t1n * d * 4 + (8 << 20)),
        name="dft1",
    )(x.reshape(batch, t1n, t2n, d), mod, m1)

    spb2 = half // units
    tok = units * 2 * t2n
    zblk = lambda off: pl.BlockSpec((None, units, t2n, d), lambda i: (i // spb2, off + i % spb2, 0, 0))
    out = pl.pallas_call(
        functools.partial(_dft23_kernel, units=units, steps_per_batch=spb2, t1n=t1n),
        grid=(batch * spb2,),
        in_specs=[
            any_spec,
            zblk(0),
            zblk(spb2),
            pl.BlockSpec((units, 4 * t2n, 2 * t2n), lambda i: (i % spb2, 0, 0)),
            pl.BlockSpec((None, 3, d), lambda i: (i // spb2, 0, 0)),
            _resident(cs.shape),
            _resident(w_out.shape),
            _resident((1, d)),
            _resident((1, d)),
        ],
        out_specs=any_spec,
        out_shape=jax.ShapeDtypeStruct((batch, t2n, t1n, d), F32),
        scratch_shapes=[pltpu.VMEM((2, units, 2, t2n, d), F32), pltpu.VMEM((2, units, 2, t2n, d), F32),
                        pltpu.SemaphoreType.DMA((2,)), pltpu.SemaphoreType.DMA((2,))],
        compiler_params=arb(4 * tok * d * 4 + 4 * tok * d * 4 + 2 * d * d * 2 + 10 * tok * d * 4 + (8 << 20)),
        name="dft23",
    )(x.reshape(batch, t2n, t1n, d), z, z, m2, mod, cs, w_out, lng, lnb)
    return out.reshape(n, d)


FFN_TM = 1024
FFN_SUB = 256
PROJ_TM = 1024
PROJ_SUB = 256
MIX_TM = 1024


def kernel(x, c, ctx, c_ctx, ada_w, ada_b, ln_g, ln_b, ffn1_w1, ffn1_w3, ffn1_w2, ffn2_w1, ffn2_w3, ffn2_w2,
           mix_w_in, pool_w, pool_scale, attn_sinks, mix_w_out, fourier_w_out):
    batch, seq_len, d = x.shape
    n_ctx = ctx.shape[1]
    assert d == D_MODEL and seq_len % GRID_W == 0
    assert seq_len % FFN_TM == 0 and seq_len % PROJ_TM == 0 and seq_len % MIX_TM == 0
    assert n_ctx % BLOCK == 0

    c_rows = jnp.zeros((2 * 8, d), F32).at[:batch].set(c).at[batch].set(c_ctx)
    mod_all = _ada(c_rows, ada_w, ada_b).reshape(DEPTH, 2 * 8, N_MOD, d)

    xf = x.reshape(batch * seq_len, d)
    cf = ctx.reshape(batch * n_ctx, d)
    bf = lambda w: w.astype(BF16)
    ffn1 = (bf(ffn1_w1), bf(ffn1_w3), bf(ffn1_w2))
    ffn2 = (bf(ffn2_w1), bf(ffn2_w3), bf(ffn2_w2))
    for l in range(DEPTH):
        mod = mod_all[l, :batch]
        lng = ln_g[l][:, None, :]
        lnb = ln_b[l][:, None, :]
        xf = _ffn(xf, mod[:, 0:3], lng[0], lnb[0], *ffn1, layer=l, seq_len=seq_len, tm=FFN_TM, sub=FFN_SUB)
        if l % 2 == 0:
            e = l // 2
            mod_c = mod_all[l, batch:batch + 1]
            cf = _ffn(cf, mod_c[:, 0:3], lng[0], lnb[0], *ffn1, layer=l, seq_len=n_ctx, tm=n_ctx, sub=n_ctx)
            w_in = bf(mix_w_in[e])
            u = _proj(xf, mod[:, 3:6], w_in, _rope_tables(seq_len), seq_len=seq_len, tm=PROJ_TM, sub=PROJ_SUB)
            u_ctx = _proj(cf, mod_c[:, 3:6], w_in, _identity_rope_tables(n_ctx), seq_len=n_ctx, tm=n_ctx, sub=n_ctx)
            xf = _mix0(xf, u, u_ctx, mod[:, 3:6], attn_sinks[e], bf(pool_w[e]), pool_scale[e][None, :],
                       bf(mix_w_out[e]), lng[1], lnb[1], seq_len=seq_len, n_ctx=n_ctx, tm=MIX_TM)
        else:
            xf = _fourier_mix(xf, mod[:, 3:6], bf(fourier_w_out[l // 2]), lng[1], lnb[1],
                              batch=batch, seq_len=seq_len)
        xf = _ffn(xf, mod[:, 6:9], lng[2], lnb[2], *ffn2, layer=l, seq_len=seq_len, tm=FFN_TM, sub=FFN_SUB)
    return xf.reshape(batch, seq_len, d)
```
